```python
import math
import jax, jax.numpy as jnp
from jax import lax
import numpy as np

D_MODEL = 1024
BATCH = 8
SEQ = 4096
DEPTH = 2

N_A = DEPTH // 2
N_B = DEPTH - N_A

EXPAND = 2
D_INNER = EXPAND * D_MODEL
SSM_HEAD_DIM = 64
SSM_HEADS = D_INNER // SSM_HEAD_DIM
N_GROUPS = 4
HEADS_PER_GROUP = SSM_HEADS // N_GROUPS
D_STATE = 128
CONV_K = 4
CONV_DIM = D_INNER + 2 * N_GROUPS * D_STATE
IN_PROJ_DIM = 2 * D_INNER + 2 * N_GROUPS * D_STATE + SSM_HEADS
CHUNK = 128
DT_MIN = 0.001
DT_MAX = 0.1

SB_HEAD_DIM = 64
SB_HEADS = D_MODEL // SB_HEAD_DIM
SB_WIDTH = SB_HEADS * SB_HEAD_DIM
SB_BLOCK = 128

PLE_DIM = 256

NORM_EPS = 1e-6

kernel_name = "yoco_mamba2_stickbreaking_hybrid"


def rms(x):
    xf = x.astype(jnp.float32)
    return (xf * lax.rsqrt(jnp.mean(xf * xf, axis=-1, keepdims=True) + NORM_EPS)).astype(x.dtype)


def rmsnorm(x, g):
    return rms(x) * g.astype(x.dtype)


def causal_depthwise_conv(u, w, b):
    k = w.shape[0]
    s = u.shape[1]
    up = jnp.pad(u, ((0, 0), (k - 1, 0), (0, 0)))
    out = b
    for j in range(k):
        out = out + w[j] * up[:, j:j + s]
    return out


def ssd_chunked_scan(xs, dt, A, Bm, Cm):
    bsz, s, _, _ = xs.shape
    nc = s // CHUNK
    G, HG, P, N, L = N_GROUPS, HEADS_PER_GROUP, SSM_HEAD_DIM, D_STATE, CHUNK
    xc = xs.astype(jnp.float32).reshape(bsz, nc, L, G, HG, P).transpose(1, 0, 2, 3, 4, 5)
    dtc = dt.reshape(bsz, nc, L, G, HG).transpose(1, 0, 2, 3, 4)
    Bc = Bm.astype(jnp.float32).reshape(bsz, nc, L, G, N).transpose(1, 0, 2, 3, 4)
    Cc = Cm.astype(jnp.float32).reshape(bsz, nc, L, G, N).transpose(1, 0, 2, 3, 4)
    Ag = A.reshape(G, HG)
    causal = jnp.tril(jnp.ones((L, L), dtype=bool))

    def step(state, inp):
        x_k, dt_k, B_k, C_k = inp
        acum = jnp.cumsum(dt_k * Ag, axis=1)
        seg = acum[:, :, None] - acum[:, None, :]
        decay = jnp.exp(jnp.where(causal[None, :, :, None, None], seg, -jnp.inf))
        xdt = x_k * dt_k[..., None]
        cb = jnp.einsum('blgn,bsgn->blsg', C_k, B_k)
        y_intra = jnp.einsum('blsgh,bsghp->blghp', cb[..., None] * decay, xdt)
        y_inter = jnp.einsum('blgn,bghpn->blghp', C_k, state) * jnp.exp(acum)[..., None]
        to_end = jnp.exp(acum[:, -1:] - acum)
        new_state = state * jnp.exp(acum[:, -1])[..., None, None] + jnp.einsum(
            'bsgn,bsghp->bghpn', B_k, xdt * to_end[..., None])
        return new_state, y_intra + y_inter

    state0 = jnp.zeros((bsz, G, HG, P, N), jnp.float32)
    _, y = lax.scan(step, state0, (xc, dtc, Bc, Cc))
    y = y.transpose(1, 0, 2, 3, 4, 5).reshape(bsz, s, SSM_HEADS, P)
    return y.astype(xs.dtype)


def mamba2_mixer(h, norm_g, w_in, conv_w, conv_b, dt_bias, A_log, d_skip, y_g, w_out):
    bsz, s, _ = h.shape
    u = rmsnorm(h, norm_g)
    zxbcdt = u @ w_in
    z = zxbcdt[..., :D_INNER]
    xbc = zxbcdt[..., D_INNER:D_INNER + CONV_DIM]
    dt = zxbcdt[..., D_INNER + CONV_DIM:]
    xbc = jax.nn.silu(causal_depthwise_conv(xbc, conv_w, conv_b))
    xs = xbc[..., :D_INNER].reshape(bsz, s, SSM_HEADS, SSM_HEAD_DIM)
    Bm = xbc[..., D_INNER:D_INNER + N_GROUPS * D_STATE].reshape(bsz, s, N_GROUPS, D_STATE)
    Cm = xbc[..., D_INNER + N_GROUPS * D_STATE:].reshape(bsz, s, N_GROUPS, D_STATE)
    dt = jax.nn.softplus(dt.astype(jnp.float32) + dt_bias.astype(jnp.float32))
    A = -jnp.exp(A_log.astype(jnp.float32))
    y = ssd_chunked_scan(xs, dt, A, Bm, Cm)
    y = y + d_skip.astype(y.dtype)[:, None] * xs
    y = y.reshape(bsz, s, D_INNER) * jax.nn.silu(z)
    y = rms(y.reshape(bsz, s, N_GROUPS, D_INNER // N_GROUPS)).reshape(bsz, s, D_INNER) * y_g
    return y @ w_out


def shared_kv(h, norm_g, w_kv, k_g):
    bsz, s, _ = h.shape
    kv = rmsnorm(h, norm_g) @ w_kv
    k = rmsnorm(kv[..., :SB_WIDTH].reshape(bsz, s, SB_HEADS, SB_HEAD_DIM), k_g)
    v = kv[..., SB_WIDTH:].reshape(bsz, s, SB_HEADS, SB_HEAD_DIM)
    return k.transpose(0, 2, 1, 3), v.transpose(0, 2, 1, 3)


def stick_breaking_mixer(h, k, v, norm_g, w_in, q_g, w_out):
    bsz, s, _ = h.shape
    qg = rmsnorm(h, norm_g) @ w_in
    gate = qg[..., SB_WIDTH:]
    q = rmsnorm(qg[..., :SB_WIDTH].reshape(bsz, s, SB_HEADS, SB_HEAD_DIM), q_g)
    q = (q * (1.0 / math.sqrt(SB_HEAD_DIM))).transpose(0, 2, 1, 3)
    outs = []
    for blk in range(s // SB_BLOCK):
        q0 = blk * SB_BLOCK
        kend = q0 + SB_BLOCK
        z = jnp.einsum('bhtd,bhsd->bhts', q[:, :, q0:kend], k[:, :, :kend]).astype(jnp.float32)
        t_idx = q0 + jnp.arange(SB_BLOCK)[:, None]
        s_idx = jnp.arange(kend)[None, :]
        strict = s_idx < t_idx
        log_keep = jnp.where(strict, jax.nn.log_sigmoid(-z), 0.0)
        suffix = lax.cumsum(log_keep, axis=3, reverse=True) - log_keep
        weights = jnp.where(strict, jnp.exp(jax.nn.log_sigmoid(z) + suffix), 0.0)
        outs.append(jnp.einsum('bhts,bhsd->bhtd', weights.astype(v.dtype), v[:, :, :kend]))
    o = jnp.concatenate(outs, axis=2).transpose(0, 2, 1, 3).reshape(bsz, s, SB_WIDTH)
    return (o * jax.nn.silu(gate)) @ w_out


def per_layer_embedding(h, p_i, norm_g, w_gate, w_proj):
    return h + (p_i @ w_proj) * jax.nn.sigmoid(rmsnorm(h, norm_g) @ w_gate)


def setup_inputs(seed: int = 0) -> dict:
    key = jax.random.key(seed)
    ks = jax.random.split(key, 24)

    def nrm(k, shape, scale):
        return jax.random.normal(k, shape, jnp.float32) * scale

    def gain(k, shape):
        return 1.0 + 0.05 * jax.random.normal(k, shape, jnp.float32)

    u = jax.random.uniform(ks[6], (N_A, SSM_HEADS), jnp.float32)
    dt = jnp.exp(u * (math.log(DT_MAX) - math.log(DT_MIN)) + math.log(DT_MIN))
    dt_bias = dt + jnp.log(-jnp.expm1(-dt))
    A_log = jnp.log(jax.random.uniform(ks[7], (N_A, SSM_HEADS), jnp.float32, minval=1.0, maxval=16.0))
    return {
        "x": nrm(ks[0], (BATCH, SEQ, D_MODEL), 1.0),
        "p": nrm(ks[1], (DEPTH, BATCH, SEQ, PLE_DIM), 1.0),
        "m_norm": gain(ks[2], (N_A, D_MODEL)),
        "m_in": nrm(ks[3], (N_A, D_MODEL, IN_PROJ_DIM), D_MODEL ** -0.5),
        "m_conv_w": nrm(ks[4], (N_A, CONV_K, CONV_DIM), CONV_K ** -0.5),
        "m_conv_b": nrm(ks[5], (N_A, CONV_DIM), 0.01),
        "m_dt_bias": dt_bias,
        "m_A_log": A_log,
        "m_D": gain(ks[8], (N_A, SSM_HEADS)),
        "m_ynorm": gain(ks[9], (N_A, D_INNER)),
        "m_out": nrm(ks[10], (N_A, D_INNER, D_MODEL), D_INNER ** -0.5),
        "kv_norm": gain(ks[11], (D_MODEL,)),
        "w_kv": nrm(ks[12], (D_MODEL, 2 * SB_WIDTH), D_MODEL ** -0.5),
        "k_norm": gain(ks[13], (SB_HEAD_DIM,)),
        "s_norm": gain(ks[14], (N_B, D_MODEL)),
        "s_in": nrm(ks[15], (N_B, D_MODEL, 2 * SB_WIDTH), D_MODEL ** -0.5),
        "q_norm": gain(ks[16], (N_B, SB_HEAD_DIM)),
        "s_out": nrm(ks[17], (N_B, SB_WIDTH, D_MODEL), SB_WIDTH ** -0.5),
        "ple_norm": gain(ks[18], (DEPTH, D_MODEL)),
        "ple_gate": nrm(ks[19], (DEPTH, D_MODEL, D_MODEL), D_MODEL ** -0.5),
        "ple_proj": nrm(ks[20], (DEPTH, PLE_DIM, D_MODEL), PLE_DIM ** -0.5),
    }


def reference(x, p, m_norm, m_in, m_conv_w, m_conv_b, m_dt_bias, m_A_log, m_D, m_ynorm, m_out,
              kv_norm, w_kv, k_norm, s_norm, s_in, q_norm, s_out, ple_norm, ple_gate, ple_proj):
    h = x
    k = v = None
    for i in range(DEPTH):
        if i < N_A:
            h = h + mamba2_mixer(h, m_norm[i], m_in[i], m_conv_w[i], m_conv_b[i], m_dt_bias[i],
                                 m_A_log[i], m_D[i], m_ynorm[i], m_out[i])
        else:
            if i == N_A:
                k, v = shared_kv(h, kv_norm, w_kv, k_norm)
            j = i - N_A
            h = h + stick_breaking_mixer(h, k, v, s_norm[j], s_in[j], q_norm[j], s_out[j])
        h = per_layer_embedding(h, p[i], ple_norm[i], ple_gate[i], ple_proj[i])
    return h
```

```python
import functools
import math

import jax
import jax.numpy as jnp
from jax import lax
from jax.experimental import pallas as pl
from jax.experimental.pallas import tpu as pltpu

F32 = jnp.float32
BF16 = jnp.bfloat16

NORM_EPS = 1e-6
LANES = 128
SSD_CHUNK = 128
SSM_HEAD_DIM = 64
SSM_STATE = 128
SSM_GROUPS = 4
CONV_K = 4
SB_HEAD_DIM = 64
CONV_HALO = 8
NEG_BIG = -1e30
VMEM_LIMIT = 56 * 1024 * 1024

ROW_TILE = 512
ATTN_TILE = 256


def _sigmoid(x):
    return 1.0 / (1.0 + jnp.exp(-x))


def _softplus(x):
    return jnp.maximum(x, 0.0) + jnp.log(1.0 + jnp.exp(-jnp.abs(x)))


def _rms_scale(x):
    return lax.rsqrt(jnp.mean(x * x, axis=-1, keepdims=True) + NORM_EPS)


def _dot(a, b):
    return jnp.dot(a, b, preferred_element_type=F32)


def _dot_nt(a, b):
    return lax.dot_general(a, b, (((1,), (1,)), ((), ())), preferred_element_type=F32)


def _dot_tn(a, b):
    return lax.dot_general(a, b, (((0,), (0,)), ((), ())), preferred_element_type=F32)


def _split_dot(x, m, terms):
    acc = None
    r = x
    for _ in range(terms):
        hi = r.astype(BF16)
        d = _dot(hi, m)
        acc = d if acc is None else acc + d
        r = r - hi.astype(F32)
    return acc


def _full(shape):
    return pl.BlockSpec(shape, lambda *_: (0,) * len(shape))


def _params(semantics):
    return pltpu.CompilerParams(dimension_semantics=semantics, vmem_limit_bytes=VMEM_LIMIT)


def _mamba_in_proj_kernel(x_ref, g_ref, wz_ref, wx_ref, wdt_ref, z_ref, xbc_ref, dt_ref):
    x = x_ref[...]
    u = (x * _rms_scale(x) * g_ref[...]).astype(BF16)
    z_ref[...] = _dot(u, wz_ref[...]).astype(z_ref.dtype)
    xbc_ref[...] = _dot(u, wx_ref[...]).astype(xbc_ref.dtype)
    dt_ref[...] = _dot(u, wdt_ref[...])


def _mamba_in_proj(x2, g, wz, wx, wdt):
    t, d = x2.shape
    row = lambda n: pl.BlockSpec((ROW_TILE, n), lambda i: (i, 0))
    return pl.pallas_call(
        _mamba_in_proj_kernel,
        grid=(t // ROW_TILE,),
        in_specs=[row(d), _full(g.shape), _full(wz.shape), _full(wx.shape), _full(wdt.shape)],
        out_specs=[row(wz.shape[1]), row(wx.shape[1]), row(wdt.shape[1])],
        out_shape=[jax.ShapeDtypeStruct((t, wz.shape[1]), BF16),
                   jax.ShapeDtypeStruct((t, wx.shape[1]), BF16),
                   jax.ShapeDtypeStruct((t, wdt.shape[1]), F32)],
        compiler_params=_params(("parallel",)),
        name="mamba_in_proj",
    )(x2, g, wz, wx, wdt)


def _mamba_ssd_kernel(z_ref, xbc_ref, dt_ref, cw_ref, cb_ref, dtb_ref, alog_ref, dskip_ref, yg_ref,
                      expand_ref, tril_ref, y_ref, ext_ref, state_ref, *, d_inner, heads_per_group):
    L = SSD_CHUNK
    gn = SSM_GROUPS * SSM_STATE
    gw = heads_per_group * SSM_HEAD_DIM

    @pl.when(pl.program_id(1) == 0)
    def _():
        ext_ref[0:CONV_HALO, :] = jnp.zeros((CONV_HALO, ext_ref.shape[1]), F32)
        state_ref[...] = jnp.zeros(state_ref.shape, F32)

    ext_ref[CONV_HALO:CONV_HALO + L, :] = xbc_ref[0].astype(F32)
    conv = cb_ref[...]
    for j in range(CONV_K):
        off = CONV_HALO - (CONV_K - 1) + j
        conv = conv + cw_ref[j:j + 1, :] * ext_ref[off:off + L, :]
    ext_ref[0:CONV_HALO, :] = ext_ref[L:L + CONV_HALO, :]
    xbc = conv * _sigmoid(conv)
    x = xbc[:, :d_inner]
    bm = xbc[:, d_inner:d_inner + gn].astype(BF16)
    cm = xbc[:, d_inner + gn:].astype(BF16)

    dt = _softplus(dt_ref[0] + dtb_ref[...])
    a = dt * (-jnp.exp(alog_ref[...]))
    acum = _cumsum_time(tril_ref[...], a)
    acum_t = acum.T
    a_last = acum[L - 1:L, :]
    w_end = dt * jnp.exp(a_last - acum)
    stacked = jnp.concatenate(
        [dt, w_end, jnp.exp(acum), jnp.broadcast_to(jnp.exp(a_last), (8, LANES))], axis=0)
    ex = _split_dot(stacked, expand_ref[...], 2)
    dt_x, w_x, e_x, dec_x = ex[0:L], ex[L:2 * L], ex[2 * L:3 * L], ex[3 * L:3 * L + 1]

    xdt = (x * dt_x).astype(BF16)
    xw = (x * w_x).astype(BF16)

    row = lax.broadcasted_iota(jnp.int32, (L, L), 0)
    col = lax.broadcasted_iota(jnp.int32, (L, L), 1)
    causal = row >= col
    low_half = col < SSM_HEAD_DIM

    for g in range(SSM_GROUPS):
        bg = bm[:, g * SSM_STATE:(g + 1) * SSM_STATE]
        cg = cm[:, g * SSM_STATE:(g + 1) * SSM_STATE]
        gcols = slice(g * gw, (g + 1) * gw)
        cb = _dot_nt(cg, bg)
        state = state_ref[g]
        y_inter = _dot(cg, state.astype(BF16)) * e_x[:, gcols]
        tiles = []
        for hp in range(heads_per_group // 2):
            h0 = g * heads_per_group + 2 * hp
            cols = slice(h0 * SSM_HEAD_DIM, (h0 + 2) * SSM_HEAD_DIM)
            xd = xdt[:, cols]
            parts = []
            for h in (h0, h0 + 1):
                seg = acum[:, h:h + 1] - acum_t[h:h + 1, :]
                decay = jnp.exp(jnp.where(causal, seg, NEG_BIG))
                parts.append(_dot((cb * decay).astype(BF16), xd))
            y = jnp.where(low_half, parts[0], parts[1])
            y = y + y_inter[:, hp * LANES:(hp + 1) * LANES] + dskip_ref[:, cols] * x[:, cols]
            zc = z_ref[0, :, cols].astype(F32)
            tiles.append(y * (zc * _sigmoid(zc)))
        state_ref[g] = state * dec_x[:, gcols] + _dot_tn(bg, xw[:, gcols])
        ss = tiles[0] * tiles[0]
        for tl in tiles[1:]:
            ss = ss + tl * tl
        scale = lax.rsqrt(jnp.sum(ss, axis=-1, keepdims=True) * (1.0 / gw) + NORM_EPS)
        for hp, tl in enumerate(tiles):
            c0 = g * gw + hp * LANES
            y_ref[0, :, c0:c0 + LANES] = (tl * scale * yg_ref[:, c0:c0 + LANES]).astype(y_ref.dtype)


def _cumsum_time(tril, a):
    acc = None
    r = a
    for _ in range(3):
        hi = r.astype(BF16)
        d = _dot(tril, hi)
        acc = d if acc is None else acc + d
        r = r - hi.astype(F32)
    return acc


def _mamba_ssd(z, xbc, dt, conv_w, conv_b, dt_bias, a_log, d_skip, y_g, *, n_heads):
    b, s, d_inner = z.shape
    conv_dim = xbc.shape[-1]
    hpg = n_heads // SSM_GROUPS
    gw = hpg * SSM_HEAD_DIM
    L = SSD_CHUNK

    pad = LANES - n_heads
    dtb = jnp.pad(dt_bias.astype(F32), (0, pad)).reshape(1, LANES)
    alog = jnp.pad(a_log.astype(F32), (0, pad)).reshape(1, LANES)
    dskip = jnp.repeat(d_skip.astype(F32), SSM_HEAD_DIM).reshape(1, d_inner)
    expand = (jnp.arange(LANES)[:, None] == (jnp.arange(d_inner)[None, :] // SSM_HEAD_DIM)).astype(BF16)
    tril = (jnp.arange(L)[:, None] >= jnp.arange(L)[None, :]).astype(BF16)

    blk = lambda n: pl.BlockSpec((1, L, n), lambda i, c: (i, c, 0))
    kern = functools.partial(_mamba_ssd_kernel, d_inner=d_inner, heads_per_group=hpg)
    return pl.pallas_call(
        kern,
        grid=(b, s // L),
        in_specs=[blk(d_inner), blk(conv_dim), blk(LANES),
                  _full(conv_w.shape), _full((1, conv_dim)), _full((1, LANES)), _full((1, LANES)),
                  _full((1, d_inner)), _full((1, d_inner)), _full(expand.shape), _full(tril.shape)],
        out_specs=blk(d_inner),
        out_shape=jax.ShapeDtypeStruct((b, s, d_inner), BF16),
        scratch_shapes=[pltpu.VMEM((L + CONV_HALO, conv_dim), F32),
                        pltpu.VMEM((SSM_GROUPS, SSM_STATE, gw), F32)],
        compiler_params=_params(("parallel", "arbitrary")),
        name="mamba_ssd",
    )(z, xbc, dt, conv_w.astype(F32), conv_b.astype(F32).reshape(1, conv_dim), dtb, alog, dskip,
      y_g.astype(F32).reshape(1, d_inner), expand, tril)


def _layer_tail_kernel(y_ref, h_ref, p_ref, wo_ref, g_ref, wg_ref, wp_ref, o_ref):
    h1 = h_ref[...] + _dot(y_ref[...], wo_ref[...])
    u = (h1 * _rms_scale(h1) * g_ref[...]).astype(BF16)
    gate = _sigmoid(_dot(u, wg_ref[...]))
    pe = _dot(p_ref[...].astype(BF16), wp_ref[...])
    o_ref[...] = h1 + pe * gate


def _layer_tail(y2, h2, p2, w_out, ple_g, w_gate, w_proj):
    t, d = h2.shape
    row = lambda n: pl.BlockSpec((ROW_TILE, n), lambda i: (i, 0))
    return pl.pallas_call(
        _layer_tail_kernel,
        grid=(t // ROW_TILE,),
        in_specs=[row(y2.shape[1]), row(d), row(p2.shape[1]), _full(w_out.shape), _full((1, d)),
                  _full(w_gate.shape), _full(w_proj.shape)],
        out_specs=row(d),
        out_shape=jax.ShapeDtypeStruct((t, d), F32),
        compiler_params=_params(("parallel",)),
        name="layer_tail",
    )(y2, h2, p2, w_out.astype(BF16), ple_g.astype(F32).reshape(1, d), w_gate.astype(BF16),
      w_proj.astype(BF16))


def _head_norm(x, gain):
    rows, width = x.shape
    low = lax.broadcasted_iota(jnp.int32, (rows, LANES), 1) < SB_HEAD_DIM
    out = []
    for j in range(width // LANES):
        blk = x[:, j * LANES:(j + 1) * LANES]
        sq = blk * blk
        s_lo = jnp.sum(jnp.where(low, sq, 0.0), axis=-1, keepdims=True)
        s_hi = jnp.sum(jnp.where(low, 0.0, sq), axis=-1, keepdims=True)
        ms = jnp.where(low, s_lo, s_hi) * (1.0 / SB_HEAD_DIM)
        out.append(blk * lax.rsqrt(ms + NORM_EPS) * gain[:, j * LANES:(j + 1) * LANES])
    return out


def _attn_in_proj_kernel(h_ref, gkv_ref, gq_ref, wk_ref, wv_ref, wq_ref, wgate_ref, kg_ref, qg_ref,
                         q_ref, k_ref, v_ref, gate_ref):
    h = h_ref[...]
    hn = h * _rms_scale(h)
    ukv = (hn * gkv_ref[...]).astype(BF16)
    uq = (hn * gq_ref[...]).astype(BF16)
    v_ref[...] = _dot(ukv, wv_ref[...]).astype(v_ref.dtype)
    gate_ref[...] = _dot(uq, wgate_ref[...]).astype(gate_ref.dtype)
    for j, blk in enumerate(_head_norm(_dot(ukv, wk_ref[...]), kg_ref[...])):
        k_ref[:, j * LANES:(j + 1) * LANES] = blk.astype(k_ref.dtype)
    for j, blk in enumerate(_head_norm(_dot(uq, wq_ref[...]), qg_ref[...])):
        q_ref[:, j * LANES:(j + 1) * LANES] = blk.astype(q_ref.dtype)


def _attn_in_proj(h2, kv_norm, s_norm, w_kv, s_in, k_norm, q_norm):
    t, d = h2.shape
    width = w_kv.shape[1] // 2
    n_heads = width // SB_HEAD_DIM
    vec = lambda v: v.astype(F32).reshape(1, -1)
    kg = vec(jnp.tile(k_norm, n_heads))
    qg = vec(jnp.tile(q_norm * (1.0 / math.sqrt(SB_HEAD_DIM)), n_heads))
    wk, wv = w_kv[:, :width].astype(BF16), w_kv[:, width:].astype(BF16)
    wq, wgate = s_in[:, :width].astype(BF16), s_in[:, width:].astype(BF16)
    row = lambda n: pl.BlockSpec((ROW_TILE, n), lambda i: (i, 0))
    out = jax.ShapeDtypeStruct((t, width), BF16)
    return pl.pallas_call(
        _attn_in_proj_kernel,
        grid=(t // ROW_TILE,),
        in_specs=[row(d), _full((1, d)), _full((1, d)), _full(wk.shape), _full(wv.shape),
                  _full(wq.shape), _full(wgate.shape), _full((1, width)), _full((1, width))],
        out_specs=[row(width)] * 4,
        out_shape=[out] * 4,
        compiler_params=_params(("parallel",)),
        name="attn_in_proj",
    )(h2, vec(kv_norm), vec(s_norm), wk, wv, wq, wgate, kg, qg)


def _sb_attention_kernel(q_ref, k_ref, v_ref, gate_ref, tri_ref, o_ref, acc_ref):
    tq = ATTN_TILE
    qi = pl.program_id(2)
    q2 = q_ref[0]
    lane = lax.broadcasted_iota(jnp.int32, (tq, LANES), 1)
    low = lane < SB_HEAD_DIM
    zero = jnp.zeros_like(q2)
    q_heads = (jnp.where(low, q2, zero), jnp.where(low, zero, q2))
    tri = tri_ref[...]
    row = lax.broadcasted_iota(jnp.int32, (tq, tq), 0)
    col = lax.broadcasted_iota(jnp.int32, (tq, tq), 1)
    strict = col < row

    def tile_step(kt, carry, diagonal):
        start = pl.multiple_of(kt * tq, tq)
        kblk = k_ref[0, pl.ds(start, tq), :]
        vblk = v_ref[0, pl.ds(start, tq), :]
        new_carry = []
        for h in range(2):
            z = _dot_nt(q_heads[h], kblk)
            nz = -z
            log_keep = jnp.minimum(nz, 0.0) - jnp.log(1.0 + jnp.exp(jnp.minimum(z, nz)))
            if diagonal:
                log_keep = jnp.where(strict, log_keep, 0.0)
            later = _dot(log_keep.astype(BF16), tri)
            logw = (z + log_keep) + later
            if diagonal:
                logw = jnp.where(strict, logw, NEG_BIG)
            pv = _dot(jnp.exp(logw).astype(BF16), vblk)
            contrib = jnp.exp(carry[h]) * pv
            if diagonal:
                acc_ref[h] = contrib
            else:
                acc_ref[h] += contrib
            new_carry.append(carry[h] + later[:, 0:1] + log_keep[:, 0:1])
        return tuple(new_carry)

    zeros = jnp.zeros((tq, 1), F32)
    carry = tile_step(qi, (zeros, zeros), True)
    lax.fori_loop(0, qi, lambda i, c: tile_step(qi - 1 - i, c, False), carry)

    o = jnp.where(low, acc_ref[0], acc_ref[1])
    gt = gate_ref[0].astype(F32)
    o_ref[0] = (o * (gt * _sigmoid(gt))).astype(o_ref.dtype)


def _sb_attention(q, k, v, gate):
    b, s, width = q.shape
    tq = ATTN_TILE
    tri = (jnp.arange(tq)[:, None] > jnp.arange(tq)[None, :]).astype(BF16)
    qblk = pl.BlockSpec((1, tq, LANES), lambda i, hp, t: (i, t, hp))
    kvblk = pl.BlockSpec((1, s, LANES), lambda i, hp, t: (i, 0, hp))
    return pl.pallas_call(
        _sb_attention_kernel,
        grid=(b, width // LANES, s // tq),
        in_specs=[qblk, kvblk, kvblk, qblk, _full(tri.shape)],
        out_specs=qblk,
        out_shape=jax.ShapeDtypeStruct((b, s, width), BF16),
        scratch_shapes=[pltpu.VMEM((2, tq, LANES), F32)],
        compiler_params=_params(("parallel", "parallel", "arbitrary")),
        name="sb_attention",
    )(q, k, v, gate, tri)


def kernel(x, p, m_norm, m_in, m_conv_w, m_conv_b, m_dt_bias, m_A_log, m_D, m_ynorm, m_out, kv_norm, w_kv,
           k_norm, s_norm, s_in, q_norm, s_out, ple_norm, ple_gate, ple_proj):
    b, s, d = x.shape
    t = b * s
    n_a = m_norm.shape[0]
    depth = p.shape[0]
    n_heads = m_dt_bias.shape[1]
    d_inner = n_heads * SSM_HEAD_DIM
    conv_dim = m_conv_w.shape[2]

    h = x.reshape(t, d)
    kv = None
    for i in range(depth):
        if i < n_a:
            w_in = m_in[i]
            wz = w_in[:, :d_inner].astype(BF16)
            wx = w_in[:, d_inner:d_inner + conv_dim].astype(BF16)
            wdt = jnp.pad(w_in[:, d_inner + conv_dim:], ((0, 0), (0, LANES - n_heads))).astype(BF16)
            z, xbc, dt = _mamba_in_proj(h, m_norm[i].astype(F32).reshape(1, d), wz, wx, wdt)
            y = _mamba_ssd(z.reshape(b, s, -1), xbc.reshape(b, s, -1), dt.reshape(b, s, -1),
                           m_conv_w[i], m_conv_b[i], m_dt_bias[i], m_A_log[i], m_D[i], m_ynorm[i],
                           n_heads=n_heads)
            y = y.reshape(t, d_inner)
            w_out = m_out[i]
        else:
            j = i - n_a
            q, k, v, gate = _attn_in_proj(h, kv_norm, s_norm[j], w_kv, s_in[j], k_norm, q_norm[j])
            if kv is None:
                kv = (k, v)
            k, v = kv
            width = q.shape[1]
            y = _sb_attention(q.reshape(b, s, width), k.reshape(b, s, width), v.reshape(b, s, width),
                              gate.reshape(b, s, width)).reshape(t, width)
            w_out = s_out[j]
        h = _layer_tail(y, h, p[i].reshape(t, -1), w_out, ple_norm[i], ple_gate[i], ple_proj[i])
    return h.reshape(b, s, d)
```

```python
import functools
import math

import jax
import jax.numpy as jnp
from jax import lax
from jax.experimental import pallas as pl
from jax.experimental.pallas import tpu as pltpu

F32 = jnp.float32
BF16 = jnp.bfloat16

NORM_EPS = 1e-6
LANES = 128
SSD_CHUNK = 128
SSM_HEAD_DIM = 64
SSM_STATE = 128
SSM_GROUPS = 4
CONV_K = 4
SB_HEAD_DIM = 64
CONV_HALO = 8
NEG_BIG = -1e30
VMEM_LIMIT = 56 * 1024 * 1024

ROW_TILE = 512
ATTN_TILE = 256
ATTN_LANE_TILES = 2
LOG2E = 1.4426950408889634
EXP_UNDERFLOW = -105.0


def _sigmoid(x):
    return 1.0 / (1.0 + jnp.exp(-x))


def _softplus(x):
    return jnp.maximum(x, 0.0) + jnp.log(1.0 + jnp.exp(-jnp.abs(x)))


def _rms_scale(x):
    return lax.rsqrt(jnp.mean(x * x, axis=-1, keepdims=True) + NORM_EPS)


def _dot(a, b):
    return jnp.dot(a, b, preferred_element_type=F32)


def _dot_nt(a, b):
    return lax.dot_general(a, b, (((1,), (1,)), ((), ())), preferred_element_type=F32)


def _dot_tn(a, b):
    return lax.dot_general(a, b, (((0,), (0,)), ((), ())), preferred_element_type=F32)


def _split_dot(x, m, terms):
    acc = None
    r = x
    for _ in range(terms):
        hi = r.astype(BF16)
        d = _dot(hi, m)
        acc = d if acc is None else acc + d
        r = r - hi.astype(F32)
    return acc


def _full(shape):
    return pl.BlockSpec(shape, lambda *_: (0,) * len(shape))


def _params(semantics):
    return pltpu.CompilerParams(dimension_semantics=semantics, vmem_limit_bytes=VMEM_LIMIT)


def _mamba_in_proj_kernel(x_ref, g_ref, wz_ref, wx_ref, wdt_ref, z_ref, xbc_ref, dt_ref):
    x = x_ref[...]
    u = (x * _rms_scale(x) * g_ref[...]).astype(BF16)
    z_ref[...] = _dot(u, wz_ref[...]).astype(z_ref.dtype)
    xbc_ref[...] = _dot(u, wx_ref[...]).astype(xbc_ref.dtype)
    dt_ref[...] = _dot(u, wdt_ref[...])


def _mamba_in_proj(x2, g, wz, wx, wdt):
    t, d = x2.shape
    row = lambda n: pl.BlockSpec((ROW_TILE, n), lambda i: (i, 0))
    return pl.pallas_call(
        _mamba_in_proj_kernel,
        grid=(t // ROW_TILE,),
        in_specs=[row(d), _full(g.shape), _full(wz.shape), _full(wx.shape), _full(wdt.shape)],
        out_specs=[row(wz.shape[1]), row(wx.shape[1]), row(wdt.shape[1])],
        out_shape=[jax.ShapeDtypeStruct((t, wz.shape[1]), BF16),
                   jax.ShapeDtypeStruct((t, wx.shape[1]), BF16),
                   jax.ShapeDtypeStruct((t, wdt.shape[1]), F32)],
        compiler_params=_params(("parallel",)),
        name="mamba_in_proj",
    )(x2, g, wz, wx, wdt)


def _mamba_ssd_kernel(z_ref, xbc_ref, dt_ref, cw_ref, cb_ref, dtb_ref, alog_ref, dskip_ref, yg_ref,
                      expand_ref, tril_ref, y_ref, ext_ref, state_ref, *, d_inner, heads_per_group):
    L = SSD_CHUNK
    gn = SSM_GROUPS * SSM_STATE
    gw = heads_per_group * SSM_HEAD_DIM

    @pl.when(pl.program_id(1) == 0)
    def _():
        ext_ref[0:CONV_HALO, :] = jnp.zeros((CONV_HALO, ext_ref.shape[1]), F32)
        state_ref[...] = jnp.zeros(state_ref.shape, F32)

    ext_ref[CONV_HALO:CONV_HALO + L, :] = xbc_ref[0].astype(F32)
    conv = cb_ref[...]
    for j in range(CONV_K):
        off = CONV_HALO - (CONV_K - 1) + j
        conv = conv + cw_ref[j:j + 1, :] * ext_ref[off:off + L, :]
    ext_ref[0:CONV_HALO, :] = ext_ref[L:L + CONV_HALO, :]
    xbc = conv * _sigmoid(conv)
    x = xbc[:, :d_inner]
    bm = xbc[:, d_inner:d_inner + gn].astype(BF16)
    cm = xbc[:, d_inner + gn:].astype(BF16)

    dt = _softplus(dt_ref[0] + dtb_ref[...])
    a = dt * (-jnp.exp(alog_ref[...]))
    acum = _cumsum_time(tril_ref[...], a)
    acum_t = acum.T
    a_last = acum[L - 1:L, :]
    w_end = dt * jnp.exp(a_last - acum)
    stacked = jnp.concatenate(
        [dt, w_end, jnp.exp(acum), jnp.broadcast_to(jnp.exp(a_last), (8, LANES))], axis=0)
    ex = _split_dot(stacked, expand_ref[...], 2)
    dt_x, w_x, e_x, dec_x = ex[0:L], ex[L:2 * L], ex[2 * L:3 * L], ex[3 * L:3 * L + 1]

    xdt = (x * dt_x).astype(BF16)
    xw = (x * w_x).astype(BF16)

    row = lax.broadcasted_iota(jnp.int32, (L, L), 0)
    col = lax.broadcasted_iota(jnp.int32, (L, L), 1)
    causal = row >= col
    low_half = col < SSM_HEAD_DIM

    for g in range(SSM_GROUPS):
        bg = bm[:, g * SSM_STATE:(g + 1) * SSM_STATE]
        cg = cm[:, g * SSM_STATE:(g + 1) * SSM_STATE]
        gcols = slice(g * gw, (g + 1) * gw)
        cb = _dot_nt(cg, bg)
        state = state_ref[g]
        y_inter = _dot(cg, state.astype(BF16)) * e_x[:, gcols]
        tiles = []
        for hp in range(heads_per_group // 2):
            h0 = g * heads_per_group + 2 * hp
            cols = slice(h0 * SSM_HEAD_DIM, (h0 + 2) * SSM_HEAD_DIM)
            xd = xdt[:, cols]
            parts = []
            for h in (h0, h0 + 1):
                seg = acum[:, h:h + 1] - acum_t[h:h + 1, :]
                decay = jnp.exp(jnp.where(causal, seg, NEG_BIG))
                parts.append(_dot((cb * decay).astype(BF16), xd))
            y = jnp.where(low_half, parts[0], parts[1])
            y = y + y_inter[:, hp * LANES:(hp + 1) * LANES] + dskip_ref[:, cols] * x[:, cols]
            zc = z_ref[0, :, cols].astype(F32)
            tiles.append(y * (zc * _sigmoid(zc)))
        state_ref[g] = state * dec_x[:, gcols] + _dot_tn(bg, xw[:, gcols])
        ss = tiles[0] * tiles[0]
        for tl in tiles[1:]:
            ss = ss + tl * tl
        scale = lax.rsqrt(jnp.sum(ss, axis=-1, keepdims=True) * (1.0 / gw) + NORM_EPS)
        for hp, tl in enumerate(tiles):
            c0 = g * gw + hp * LANES
            y_ref[0, :, c0:c0 + LANES] = (tl * scale * yg_ref[:, c0:c0 + LANES]).astype(y_ref.dtype)


def _cumsum_time(tril, a):
    acc = None
    r = a
    for _ in range(3):
        hi = r.astype(BF16)
        d = _dot(tril, hi)
        acc = d if acc is None else acc + d
        r = r - hi.astype(F32)
    return acc


def _mamba_ssd(z, xbc, dt, conv_w, conv_b, dt_bias, a_log, d_skip, y_g, *, n_heads):
    b, s, d_inner = z.shape
    conv_dim = xbc.shape[-1]
    hpg = n_heads // SSM_GROUPS
    gw = hpg * SSM_HEAD_DIM
    L = SSD_CHUNK

    pad = LANES - n_heads
    dtb = jnp.pad(dt_bias.astype(F32), (0, pad)).reshape(1, LANES)
    alog = jnp.pad(a_log.astype(F32), (0, pad)).reshape(1, LANES)
    dskip = jnp.repeat(d_skip.astype(F32), SSM_HEAD_DIM).reshape(1, d_inner)
    expand = (jnp.arange(LANES)[:, None] == (jnp.arange(d_inner)[None, :] // SSM_HEAD_DIM)).astype(BF16)
    tril = (jnp.arange(L)[:, None] >= jnp.arange(L)[None, :]).astype(BF16)

    blk = lambda n: pl.BlockSpec((1, L, n), lambda i, c: (i, c, 0))
    kern = functools.partial(_mamba_ssd_kernel, d_inner=d_inner, heads_per_group=hpg)
    return pl.pallas_call(
        kern,
        grid=(b, s // L),
        in_specs=[blk(d_inner), blk(conv_dim), blk(LANES),
                  _full(conv_w.shape), _full((1, conv_dim)), _full((1, LANES)), _full((1, LANES)),
                  _full((1, d_inner)), _full((1, d_inner)), _full(expand.shape), _full(tril.shape)],
        out_specs=blk(d_inner),
        out_shape=jax.ShapeDtypeStruct((b, s, d_inner), BF16),
        scratch_shapes=[pltpu.VMEM((L + CONV_HALO, conv_dim), F32),
                        pltpu.VMEM((SSM_GROUPS, SSM_STATE, gw), F32)],
        compiler_params=_params(("parallel", "arbitrary")),
        name="mamba_ssd",
    )(z, xbc, dt, conv_w.astype(F32), conv_b.astype(F32).reshape(1, conv_dim), dtb, alog, dskip,
      y_g.astype(F32).reshape(1, d_inner), expand, tril)


def _layer_tail_kernel(y_ref, h_ref, p_ref, wo_ref, g_ref, wg_ref, wp_ref, o_ref):
    h1 = h_ref[...] + _dot(y_ref[...], wo_ref[...])
    u = (h1 * _rms_scale(h1) * g_ref[...]).astype(BF16)
    gate = _sigmoid(_dot(u, wg_ref[...]))
    pe = _dot(p_ref[...].astype(BF16), wp_ref[...])
    o_ref[...] = h1 + pe * gate


def _layer_tail(y2, h2, p2, w_out, ple_g, w_gate, w_proj):
    t, d = h2.shape
    row = lambda n: pl.BlockSpec((ROW_TILE, n), lambda i: (i, 0))
    return pl.pallas_call(
        _layer_tail_kernel,
        grid=(t // ROW_TILE,),
        in_specs=[row(y2.shape[1]), row(d), row(p2.shape[1]), _full(w_out.shape), _full((1, d)),
                  _full(w_gate.shape), _full(w_proj.shape)],
        out_specs=row(d),
        out_shape=jax.ShapeDtypeStruct((t, d), F32),
        compiler_params=_params(("parallel",)),
        name="layer_tail",
    )(y2, h2, p2, w_out.astype(BF16), ple_g.astype(F32).reshape(1, d), w_gate.astype(BF16),
      w_proj.astype(BF16))


def _head_norm(x, gain):
    rows, width = x.shape
    low = lax.broadcasted_iota(jnp.int32, (rows, LANES), 1) < SB_HEAD_DIM
    out = []
    for j in range(width // LANES):
        blk = x[:, j * LANES:(j + 1) * LANES]
        sq = blk * blk
        s_lo = jnp.sum(jnp.where(low, sq, 0.0), axis=-1, keepdims=True)
        s_hi = jnp.sum(jnp.where(low, 0.0, sq), axis=-1, keepdims=True)
        ms = jnp.where(low, s_lo, s_hi) * (1.0 / SB_HEAD_DIM)
        out.append(blk * lax.rsqrt(ms + NORM_EPS) * gain[:, j * LANES:(j + 1) * LANES])
    return out


def _attn_in_proj_kernel(h_ref, gkv_ref, gq_ref, wk_ref, wv_ref, wq_ref, wgate_ref, kg_ref, qg_ref,
                         q_ref, k_ref, v_ref, gate_ref):
    h = h_ref[...]
    hn = h * _rms_scale(h)
    ukv = (hn * gkv_ref[...]).astype(BF16)
    uq = (hn * gq_ref[...]).astype(BF16)
    v_ref[...] = _dot(ukv, wv_ref[...]).astype(v_ref.dtype)
    gate_ref[...] = _dot(uq, wgate_ref[...]).astype(gate_ref.dtype)
    for j, blk in enumerate(_head_norm(_dot(ukv, wk_ref[...]), kg_ref[...])):
        k_ref[:, j * LANES:(j + 1) * LANES] = blk.astype(k_ref.dtype)
    for j, blk in enumerate(_head_norm(_dot(uq, wq_ref[...]), qg_ref[...])):
        q_ref[:, j * LANES:(j + 1) * LANES] = blk.astype(q_ref.dtype)


def _attn_in_proj(h2, kv_norm, s_norm, w_kv, s_in, k_norm, q_norm):
    t, d = h2.shape
    width = w_kv.shape[1] // 2
    n_heads = width // SB_HEAD_DIM
    vec = lambda v: v.astype(F32).reshape(1, -1)
    kg = vec(jnp.tile(k_norm, n_heads))
    qg = vec(jnp.tile(q_norm * (1.0 / math.sqrt(SB_HEAD_DIM)), n_heads))
    wk, wv = w_kv[:, :width].astype(BF16), w_kv[:, width:].astype(BF16)
    wq, wgate = s_in[:, :width].astype(BF16), s_in[:, width:].astype(BF16)
    row = lambda n: pl.BlockSpec((ROW_TILE, n), lambda i: (i, 0))
    out = jax.ShapeDtypeStruct((t, width), BF16)
    return pl.pallas_call(
        _attn_in_proj_kernel,
        grid=(t // ROW_TILE,),
        in_specs=[row(d), _full((1, d)), _full((1, d)), _full(wk.shape), _full(wv.shape),
                  _full(wq.shape), _full(wgate.shape), _full((1, width)), _full((1, width))],
        out_specs=[row(width)] * 4,
        out_shape=[out] * 4,
        compiler_params=_params(("parallel",)),
        name="attn_in_proj",
    )(h2, vec(kv_norm), vec(s_norm), wk, wv, wq, wgate, kg, qg)


def _sb_attention_kernel(q_ref, k_ref, v_ref, gate_ref, tri_ref, o_ref, acc_ref):
    tq = ATTN_TILE
    n_heads = 2 * ATTN_LANE_TILES
    qi = pl.program_id(2)
    low = lax.broadcasted_iota(jnp.int32, (tq, LANES), 1) < SB_HEAD_DIM
    q_heads = []
    for lt in range(ATTN_LANE_TILES):
        q2 = q_ref[0, :, lt * LANES:(lt + 1) * LANES]
        zero = jnp.zeros_like(q2)
        q_heads += [jnp.where(low, q2, zero), jnp.where(low, zero, q2)]
    tri = tri_ref[...]
    row = lax.broadcasted_iota(jnp.int32, (tq, tq), 0)
    col = lax.broadcasted_iota(jnp.int32, (tq, tq), 1)
    strict = col < row

    lanes = [slice((h // 2) * LANES, (h // 2 + 1) * LANES) for h in range(n_heads)]

    def tile_step(kt, carry, diagonal):
        rows = pl.ds(pl.multiple_of(kt * tq, tq), tq)
        heads = range(n_heads)
        z = [_dot_nt(q_heads[h], k_ref[0, rows, lanes[h]]) for h in heads]
        sp = [jnp.maximum(z[h], 0.0) + jnp.log(1.0 + jnp.exp2(jnp.abs(z[h]) * -LOG2E)) for h in heads]
        if diagonal:
            sp = [jnp.where(strict, s, 0.0) for s in sp]
        new_carry = tuple(carry[h] - jnp.sum(sp[h], axis=-1, keepdims=True) for h in heads)
        later = [_dot(sp[h].astype(BF16), tri) for h in heads]
        logw = [(z[h] - sp[h]) - later[h] for h in heads]
        if diagonal:
            logw = [jnp.where(strict, lw, NEG_BIG) for lw in logw]
        pv = [_dot(jnp.exp(logw[h]).astype(BF16), v_ref[0, rows, lanes[h]]) for h in heads]
        for h in heads:
            contrib = jnp.exp(carry[h]) * pv[h]
            if diagonal:
                acc_ref[h] = contrib
            else:
                acc_ref[h] += contrib
        return new_carry, jnp.max(functools.reduce(jnp.maximum, new_carry))

    def keep_going(state):
        i, live, _ = state
        return jnp.logical_and(i < qi, live > EXP_UNDERFLOW)

    def walk(state):
        i, _, carry = state
        carry, live = tile_step(qi - 1 - i, carry, False)
        return i + 1, live, carry

    zeros = jnp.zeros((tq, 1), F32)
    carry, live = tile_step(qi, (zeros,) * n_heads, True)
    lax.while_loop(keep_going, walk, (jnp.int32(0), live, carry))

    for lt in range(ATTN_LANE_TILES):
        lanes = slice(lt * LANES, (lt + 1) * LANES)
        o = jnp.where(low, acc_ref[2 * lt], acc_ref[2 * lt + 1])
        gt = gate_ref[0, :, lanes].astype(F32)
        o_ref[0, :, lanes] = (o * (gt * _sigmoid(gt))).astype(o_ref.dtype)


def _sb_attention(q, k, v, gate):
    b, s, width = q.shape
    tq = ATTN_TILE
    gw = ATTN_LANE_TILES * LANES
    tri = (jnp.arange(tq)[:, None] > jnp.arange(tq)[None, :]).astype(BF16)
    qblk = pl.BlockSpec((1, tq, gw), lambda i, hg, t: (i, t, hg))
    kvblk = pl.BlockSpec((1, s, gw), lambda i, hg, t: (i, 0, hg))
    return pl.pallas_call(
        _sb_attention_kernel,
        grid=(b, width // gw, s // tq),
        in_specs=[qblk, kvblk, kvblk, qblk, _full(tri.shape)],
        out_specs=qblk,
        out_shape=jax.ShapeDtypeStruct((b, s, width), BF16),
        scratch_shapes=[pltpu.VMEM((2 * ATTN_LANE_TILES, tq, LANES), F32)],
        compiler_params=_params(("parallel", "parallel", "arbitrary")),
        name="sb_attention",
    )(q, k, v, gate, tri)


def kernel(x, p, m_norm, m_in, m_conv_w, m_conv_b, m_dt_bias, m_A_log, m_D, m_ynorm, m_out, kv_norm, w_kv,
           k_norm, s_norm, s_in, q_norm, s_out, ple_norm, ple_gate, ple_proj):
    b, s, d = x.shape
    t = b * s
    n_a = m_norm.shape[0]
    depth = p.shape[0]
    n_heads = m_dt_bias.shape[1]
    d_inner = n_heads * SSM_HEAD_DIM
    conv_dim = m_conv_w.shape[2]

    h = x.reshape(t, d)
    kv = None
    for i in range(depth):
        if i < n_a:
            w_in = m_in[i]
            wz = w_in[:, :d_inner].astype(BF16)
            wx = w_in[:, d_inner:d_inner + conv_dim].astype(BF16)
            wdt = jnp.pad(w_in[:, d_inner + conv_dim:], ((0, 0), (0, LANES - n_heads))).astype(BF16)
            z, xbc, dt = _mamba_in_proj(h, m_norm[i].astype(F32).reshape(1, d), wz, wx, wdt)
            y = _mamba_ssd(z.reshape(b, s, -1), xbc.reshape(b, s, -1), dt.reshape(b, s, -1),
                           m_conv_w[i], m_conv_b[i], m_dt_bias[i], m_A_log[i], m_D[i], m_ynorm[i],
                           n_heads=n_heads)
            y = y.reshape(t, d_inner)
            w_out = m_out[i]
        else:
            j = i - n_a
            q, k, v, gate = _attn_in_proj(h, kv_norm, s_norm[j], w_kv, s_in[j], k_norm, q_norm[j])
            if kv is None:
                kv = (k, v)
            k, v = kv
            width = q.shape[1]
            y = _sb_attention(q.reshape(b, s, width), k.reshape(b, s, width), v.reshape(b, s, width),
                              gate.reshape(b, s, width)).reshape(t, width)
            w_out = s_out[j]
        h = _layer_tail(y, h, p[i].reshape(t, -1), w_out, ple_norm[i], ple_gate[i], ple_proj[i])
    return h.reshape(b, s, d)
```

```python
import functools
import math

import jax
import jax.numpy as jnp
from jax import lax
from jax.experimental import pallas as pl
from jax.experimental.pallas import tpu as pltpu

F32 = jnp.float32
BF16 = jnp.bfloat16

NORM_EPS = 1e-6
LANES = 128
SSD_CHUNK = 128
SSM_HEAD_DIM = 64
SSM_STATE = 128
SSM_GROUPS = 4
CONV_K = 4
SB_HEAD_DIM = 64
NEG_BIG = -1e30
VMEM_LIMIT = 56 * 1024 * 1024

ROW_TILE = 512
ATTN_TILE = 256
ATTN_LANE_TILES = 2
LOG2E = 1.4426950408889634
EXP_UNDERFLOW = -105.0


def _sigmoid(x):
    return 1.0 / (1.0 + jnp.exp(-x))


def _softplus(x):
    return jnp.maximum(x, 0.0) + jnp.log(1.0 + jnp.exp(-jnp.abs(x)))


def _rms_scale(x):
    return lax.rsqrt(jnp.mean(x * x, axis=-1, keepdims=True) + NORM_EPS)


def _dot(a, b):
    return jnp.dot(a, b, preferred_element_type=F32)


def _dot_nt(a, b):
    return lax.dot_general(a, b, (((1,), (1,)), ((), ())), preferred_element_type=F32)


def _dot_tn(a, b):
    return lax.dot_general(a, b, (((0,), (0,)), ((), ())), preferred_element_type=F32)


def _full(shape):
    return pl.BlockSpec(shape, lambda *_: (0,) * len(shape))


def _params(semantics):
    return pltpu.CompilerParams(dimension_semantics=semantics, vmem_limit_bytes=VMEM_LIMIT)


def _mamba_in_proj_kernel(x_ref, g_ref, wz_ref, wx_ref, wdt_ref, z_ref, xbc_ref, dt_ref):
    x = x_ref[...]
    u = (x * _rms_scale(x) * g_ref[...]).astype(BF16)
    z_ref[...] = _dot(u, wz_ref[...]).astype(z_ref.dtype)
    xbc_ref[...] = _dot(u, wx_ref[...]).astype(xbc_ref.dtype)
    dt_ref[...] = _dot(u, wdt_ref[...])


def _mamba_in_proj(x2, g, wz, wx, wdt):
    t, d = x2.shape
    row = lambda n: pl.BlockSpec((ROW_TILE, n), lambda i: (i, 0))
    return pl.pallas_call(
        _mamba_in_proj_kernel,
        grid=(t // ROW_TILE,),
        in_specs=[row(d), _full(g.shape), _full(wz.shape), _full(wx.shape), _full(wdt.shape)],
        out_specs=[row(wz.shape[1]), row(wx.shape[1]), row(wdt.shape[1])],
        out_shape=[jax.ShapeDtypeStruct((t, wz.shape[1]), BF16),
                   jax.ShapeDtypeStruct((t, wx.shape[1]), BF16),
                   jax.ShapeDtypeStruct((t, wdt.shape[1]), F32)],
        compiler_params=_params(("parallel",)),
        name="mamba_in_proj",
    )(x2, g, wz, wx, wdt)


def _mamba_ssd_kernel(z_ref, xbc_ref, dt_ref, cw_ref, cb_ref, dtb_ref, alog_ref, dskip_ref, yg_ref,
                      expand_ref, tril_ref, shift_ref, y_ref, ext_ref, state_ref, *, d_inner,
                      heads_per_group):
    L = SSD_CHUNK
    gn = SSM_GROUPS * SSM_STATE
    gw = heads_per_group * SSM_HEAD_DIM

    @pl.when(pl.program_id(1) == 0)
    def _():
        ext_ref[0:L, :] = jnp.zeros((L, ext_ref.shape[1]), ext_ref.dtype)
        state_ref[...] = jnp.zeros(state_ref.shape, F32)

    cur = xbc_ref[0]
    ext_ref[L:2 * L, :] = cur
    conv = cb_ref[...] + cw_ref[CONV_K - 1:CONV_K, :] * cur.astype(F32)
    for j in range(CONV_K - 1):
        conv = conv + cw_ref[j:j + 1, :] * _dot(shift_ref[j], ext_ref[...])
    ext_ref[0:L, :] = cur
    xbc = conv * _sigmoid(conv)
    x = xbc[:, :d_inner]
    bm = xbc[:, d_inner:d_inner + gn].astype(BF16)
    cm = xbc[:, d_inner + gn:].astype(BF16)

    dt = _softplus(dt_ref[0] + dtb_ref[...])
    a = dt * (-jnp.exp(alog_ref[...]))
    acum = _cumsum_time(tril_ref[...], a)
    acum_t = acum.T
    a_last = acum[L - 1:L, :]
    w_end = dt * jnp.exp(a_last - acum)
    stacked = jnp.concatenate(
        [dt, w_end, jnp.exp(acum), jnp.broadcast_to(jnp.exp(a_last), (8, LANES))], axis=0)
    hi = stacked.astype(BF16)
    lo = (stacked - hi.astype(F32)).astype(BF16)
    ex = _dot(jnp.concatenate([hi, lo], axis=1), expand_ref[...])
    dt_x, w_x, e_x, dec_x = ex[0:L], ex[L:2 * L], ex[2 * L:3 * L], ex[3 * L:3 * L + 1]

    xdt = (x * dt_x).astype(BF16)
    xw = (x * w_x).astype(BF16)

    row = lax.broadcasted_iota(jnp.int32, (L, L), 0)
    col = lax.broadcasted_iota(jnp.int32, (L, L), 1)
    causal = row >= col
    low_half = col < SSM_HEAD_DIM

    for g in range(SSM_GROUPS):
        bg = bm[:, g * SSM_STATE:(g + 1) * SSM_STATE]
        cg = cm[:, g * SSM_STATE:(g + 1) * SSM_STATE]
        gcols = slice(g * gw, (g + 1) * gw)
        cb = _dot_nt(cg, bg)
        state = state_ref[g]
        y_inter = _dot(cg, state.astype(BF16)) * e_x[:, gcols]
        tiles = []
        for hp in range(heads_per_group // 2):
            h0 = g * heads_per_group + 2 * hp
            cols = slice(h0 * SSM_HEAD_DIM, (h0 + 2) * SSM_HEAD_DIM)
            xd = xdt[:, cols]
            parts = []
            for h in (h0, h0 + 1):
                seg = acum[:, h:h + 1] - acum_t[h:h + 1, :]
                decay = jnp.exp(jnp.where(causal, seg, NEG_BIG))
                parts.append(_dot((cb * decay).astype(BF16), xd))
            y = jnp.where(low_half, parts[0], parts[1])
            y = y + y_inter[:, hp * LANES:(hp + 1) * LANES] + dskip_ref[:, cols] * x[:, cols]
            zc = z_ref[0, :, cols].astype(F32)
            tiles.append(y * (zc * _sigmoid(zc)))
        state_ref[g] = state * dec_x[:, gcols] + _dot_tn(bg, xw[:, gcols])
        ss = tiles[0] * tiles[0]
        for tl in tiles[1:]:
            ss = ss + tl * tl
        scale = lax.rsqrt(jnp.sum(ss, axis=-1, keepdims=True) * (1.0 / gw) + NORM_EPS)
        for hp, tl in enumerate(tiles):
            c0 = g * gw + hp * LANES
            y_ref[0, :, c0:c0 + LANES] = (tl * scale * yg_ref[:, c0:c0 + LANES]).astype(y_ref.dtype)


def _cumsum_time(tril, a):
    acc = None
    r = a
    for _ in range(3):
        hi = r.astype(BF16)
        d = _dot(tril, hi)
        acc = d if acc is None else acc + d
        r = r - hi.astype(F32)
    return acc


def _mamba_ssd(z, xbc, dt, conv_w, conv_b, dt_bias, a_log, d_skip, y_g, *, n_heads):
    b, s, d_inner = z.shape
    conv_dim = xbc.shape[-1]
    hpg = n_heads // SSM_GROUPS
    gw = hpg * SSM_HEAD_DIM
    L = SSD_CHUNK

    pad = LANES - n_heads
    dtb = jnp.pad(dt_bias.astype(F32), (0, pad)).reshape(1, LANES)
    alog = jnp.pad(a_log.astype(F32), (0, pad)).reshape(1, LANES)
    dskip = jnp.repeat(d_skip.astype(F32), SSM_HEAD_DIM).reshape(1, d_inner)
    expand = (jnp.arange(LANES)[:, None] == (jnp.arange(d_inner)[None, :] // SSM_HEAD_DIM)).astype(BF16)
    expand = jnp.concatenate([expand, expand], axis=0)
    tril = (jnp.arange(L)[:, None] >= jnp.arange(L)[None, :]).astype(BF16)
    t_idx = jnp.arange(L)[:, None]
    shift = jnp.stack([(jnp.arange(2 * L)[None, :] == L + t_idx - (CONV_K - 1 - j)).astype(BF16)
                       for j in range(CONV_K - 1)])

    blk = lambda n: pl.BlockSpec((1, L, n), lambda i, c: (i, c, 0))
    kern = functools.partial(_mamba_ssd_kernel, d_inner=d_inner, heads_per_group=hpg)
    return pl.pallas_call(
        kern,
        grid=(b, s // L),
        in_specs=[blk(d_inner), blk(conv_dim), blk(LANES),
                  _full(conv_w.shape), _full((1, conv_dim)), _full((1, LANES)), _full((1, LANES)),
                  _full((1, d_inner)), _full((1, d_inner)), _full(expand.shape), _full(tril.shape),
                  _full(shift.shape)],
        out_specs=blk(d_inner),
        out_shape=jax.ShapeDtypeStruct((b, s, d_inner), BF16),
        scratch_shapes=[pltpu.VMEM((2 * L, conv_dim), BF16),
                        pltpu.VMEM((SSM_GROUPS, SSM_STATE, gw), F32)],
        compiler_params=_params(("parallel", "arbitrary")),
        name="mamba_ssd",
    )(z, xbc, dt, conv_w.astype(F32), conv_b.astype(F32).reshape(1, conv_dim), dtb, alog, dskip,
      y_g.astype(F32).reshape(1, d_inner), expand, tril, shift)


def _layer_tail_kernel(y_ref, h_ref, p_ref, wo_ref, g_ref, wg_ref, wp_ref, o_ref):
    h1 = h_ref[...] + _dot(y_ref[...], wo_ref[...])
    u = (h1 * _rms_scale(h1) * g_ref[...]).astype(BF16)
    gate = _sigmoid(_dot(u, wg_ref[...]))
    pe = _dot(p_ref[...].astype(BF16), wp_ref[...])
    o_ref[...] = h1 + pe * gate


def _layer_tail(y2, h2, p2, w_out, ple_g, w_gate, w_proj):
    t, d = h2.shape
    row = lambda n: pl.BlockSpec((ROW_TILE, n), lambda i: (i, 0))
    return pl.pallas_call(
        _layer_tail_kernel,
        grid=(t // ROW_TILE,),
        in_specs=[row(y2.shape[1]), row(d), row(p2.shape[1]), _full(w_out.shape), _full((1, d)),
                  _full(w_gate.shape), _full(w_proj.shape)],
        out_specs=row(d),
        out_shape=jax.ShapeDtypeStruct((t, d), F32),
        compiler_params=_params(("parallel",)),
        name="layer_tail",
    )(y2, h2, p2, w_out.astype(BF16), ple_g.astype(F32).reshape(1, d), w_gate.astype(BF16),
      w_proj.astype(BF16))


def _head_norm(x, gain):
    rows, width = x.shape
    low = lax.broadcasted_iota(jnp.int32, (rows, LANES), 1) < SB_HEAD_DIM
    out = []
    for j in range(width // LANES):
        blk = x[:, j * LANES:(j + 1) * LANES]
        sq = blk * blk
        s_lo = jnp.sum(jnp.where(low, sq, 0.0), axis=-1, keepdims=True)
        s_hi = jnp.sum(jnp.where(low, 0.0, sq), axis=-1, keepdims=True)
        ms = jnp.where(low, s_lo, s_hi) * (1.0 / SB_HEAD_DIM)
        out.append(blk * lax.rsqrt(ms + NORM_EPS) * gain[:, j * LANES:(j + 1) * LANES])
    return out


def _attn_in_proj_kernel(h_ref, gkv_ref, gq_ref, wk_ref, wv_ref, wq_ref, wgate_ref, kg_ref, qg_ref,
                         q_ref, k_ref, v_ref, gate_ref):
    h = h_ref[...]
    hn = h * _rms_scale(h)
    ukv = (hn * gkv_ref[...]).astype(BF16)
    uq = (hn * gq_ref[...]).astype(BF16)
    v_ref[...] = _dot(ukv, wv_ref[...]).astype(v_ref.dtype)
    gate_ref[...] = _dot(uq, wgate_ref[...]).astype(gate_ref.dtype)
    for j, blk in enumerate(_head_norm(_dot(ukv, wk_ref[...]), kg_ref[...])):
        k_ref[:, j * LANES:(j + 1) * LANES] = blk.astype(k_ref.dtype)
    for j, blk in enumerate(_head_norm(_dot(uq, wq_ref[...]), qg_ref[...])):
        q_ref[:, j * LANES:(j + 1) * LANES] = blk.astype(q_ref.dtype)


def _attn_in_proj(h2, kv_norm, s_norm, w_kv, s_in, k_norm, q_norm):
    t, d = h2.shape
    width = w_kv.shape[1] // 2
    n_heads = width // SB_HEAD_DIM
    vec = lambda v: v.astype(F32).reshape(1, -1)
    kg = vec(jnp.tile(k_norm, n_heads))
    qg = vec(jnp.tile(q_norm * (1.0 / math.sqrt(SB_HEAD_DIM)), n_heads))
    wk, wv = w_kv[:, :width].astype(BF16), w_kv[:, width:].astype(BF16)
    wq, wgate = s_in[:, :width].astype(BF16), s_in[:, width:].astype(BF16)
    row = lambda n: pl.BlockSpec((ROW_TILE, n), lambda i: (i, 0))
    out = jax.ShapeDtypeStruct((t, width), BF16)
    return pl.pallas_call(
        _attn_in_proj_kernel,
        grid=(t // ROW_TILE,),
        in_specs=[row(d), _full((1, d)), _full((1, d)), _full(wk.shape), _full(wv.shape),
                  _full(wq.shape), _full(wgate.shape), _full((1, width)), _full((1, width))],
        out_specs=[row(width)] * 4,
        out_shape=[out] * 4,
        compiler_params=_params(("parallel",)),
        name="attn_in_proj",
    )(h2, vec(kv_norm), vec(s_norm), wk, wv, wq, wgate, kg, qg)


def _sb_attention_kernel(q_ref, k_ref, v_ref, gate_ref, tri_ref, o_ref, acc_ref):
    tq = ATTN_TILE
    n_heads = 2 * ATTN_LANE_TILES
    qi = pl.program_id(2)
    low = lax.broadcasted_iota(jnp.int32, (tq, LANES), 1) < SB_HEAD_DIM
    q_heads = []
    for lt in range(ATTN_LANE_TILES):
        q2 = q_ref[0, :, lt * LANES:(lt + 1) * LANES]
        zero = jnp.zeros_like(q2)
        q_heads += [jnp.where(low, q2, zero), jnp.where(low, zero, q2)]
    tri = tri_ref[...]
    row = lax.broadcasted_iota(jnp.int32, (tq, tq), 0)
    col = lax.broadcasted_iota(jnp.int32, (tq, tq), 1)
    strict = col < row

    lanes = [slice((h // 2) * LANES, (h // 2 + 1) * LANES) for h in range(n_heads)]

    def tile_step(kt, carry, diagonal):
        rows = pl.ds(pl.multiple_of(kt * tq, tq), tq)
        heads = range(n_heads)
        z = [_dot_nt(q_heads[h], k_ref[0, rows, lanes[h]]) for h in heads]
        sp = [jnp.maximum(z[h], 0.0) + jnp.log(1.0 + jnp.exp2(jnp.abs(z[h]) * -LOG2E)) for h in heads]
        if diagonal:
            sp = [jnp.where(strict, s, 0.0) for s in sp]
        new_carry = tuple(carry[h] - jnp.sum(sp[h], axis=-1, keepdims=True) for h in heads)
        later = [_dot(sp[h].astype(BF16), tri) for h in heads]
        logw = [(z[h] - sp[h]) - later[h] for h in heads]
        if diagonal:
            logw = [jnp.where(strict, lw, NEG_BIG) for lw in logw]
        pv = [_dot(jnp.exp(logw[h]).astype(BF16), v_ref[0, rows, lanes[h]]) for h in heads]
        for h in heads:
            contrib = jnp.exp(carry[h]) * pv[h]
            if diagonal:
                acc_ref[h] = contrib
            else:
                acc_ref[h] += contrib
        return new_carry, jnp.max(functools.reduce(jnp.maximum, new_carry))

    def keep_going(state):
        i, live, _ = state
        return jnp.logical_and(i < qi, live > EXP_UNDERFLOW)

    def walk(state):
        i, _, carry = state
        carry, live = tile_step(qi - 1 - i, carry, False)
        return i + 1, live, carry

    zeros = jnp.zeros((tq, 1), F32)
    carry, live = tile_step(qi, (zeros,) * n_heads, True)
    lax.while_loop(keep_going, walk, (jnp.int32(0), live, carry))

    for lt in range(ATTN_LANE_TILES):
        lanes = slice(lt * LANES, (lt + 1) * LANES)
        o = jnp.where(low, acc_ref[2 * lt], acc_ref[2 * lt + 1])
        gt = gate_ref[0, :, lanes].astype(F32)
        o_ref[0, :, lanes] = (o * (gt * _sigmoid(gt))).astype(o_ref.dtype)


def _sb_attention(q, k, v, gate):
    b, s, width = q.shape
    tq = ATTN_TILE
    gw = ATTN_LANE_TILES * LANES
    tri = (jnp.arange(tq)[:, None] > jnp.arange(tq)[None, :]).astype(BF16)
    qblk = pl.BlockSpec((1, tq, gw), lambda i, hg, t: (i, t, hg))
    kvblk = pl.BlockSpec((1, s, gw), lambda i, hg, t: (i, 0, hg))
    return pl.pallas_call(
        _sb_attention_kernel,
        grid=(b, width // gw, s // tq),
        in_specs=[qblk, kvblk, kvblk, qblk, _full(tri.shape)],
        out_specs=qblk,
        out_shape=jax.ShapeDtypeStruct((b, s, width), BF16),
        scratch_shapes=[pltpu.VMEM((2 * ATTN_LANE_TILES, tq, LANES), F32)],
        compiler_params=_params(("parallel", "parallel", "arbitrary")),
        name="sb_attention",
    )(q, k, v, gate, tri)


def kernel(x, p, m_norm, m_in, m_conv_w, m_conv_b, m_dt_bias, m_A_log, m_D, m_ynorm, m_out, kv_norm, w_kv,
           k_norm, s_norm, s_in, q_norm, s_out, ple_norm, ple_gate, ple_proj):
    b, s, d = x.shape
    t = b * s
    n_a = m_norm.shape[0]
    depth = p.shape[0]
    n_heads = m_dt_bias.shape[1]
    d_inner = n_heads * SSM_HEAD_DIM
    conv_dim = m_conv_w.shape[2]

    h = x.reshape(t, d)
    kv = None
    for i in range(depth):
        if i < n_a:
            w_in = m_in[i]
            wz = w_in[:, :d_inner].astype(BF16)
            wx = w_in[:, d_inner:d_inner + conv_dim].astype(BF16)
            wdt = jnp.pad(w_in[:, d_inner + conv_dim:], ((0, 0), (0, LANES - n_heads))).astype(BF16)
            z, xbc, dt = _mamba_in_proj(h, m_norm[i].astype(F32).reshape(1, d), wz, wx, wdt)
            y = _mamba_ssd(z.reshape(b, s, -1), xbc.reshape(b, s, -1), dt.reshape(b, s, -1),
                           m_conv_w[i], m_conv_b[i], m_dt_bias[i], m_A_log[i], m_D[i], m_ynorm[i],
                           n_heads=n_heads)
            y = y.reshape(t, d_inner)
            w_out = m_out[i]
        else:
            j = i - n_a
            q, k, v, gate = _attn_in_proj(h, kv_norm, s_norm[j], w_kv, s_in[j], k_norm, q_norm[j])
            if kv is None:
                kv = (k, v)
            k, v = kv
            width = q.shape[1]
            y = _sb_attention(q.reshape(b, s, width), k.reshape(b, s, width), v.reshape(b, s, width),
                              gate.reshape(b, s, width)).reshape(t, width)
            w_out = s_out[j]
        h = _layer_tail(y, h, p[i].reshape(t, -1), w_out, ple_norm[i], ple_gate[i], ple_proj[i])
    return h.reshape(b, s, d)
```

```python
import functools
import math

import jax
import jax.numpy as jnp
from jax import lax
from jax.experimental import pallas as pl
from jax.experimental.pallas import tpu as pltpu

F32 = jnp.float32
BF16 = jnp.bfloat16

NORM_EPS = 1e-6
LANES = 128
SSD_CHUNK = 128
SSM_HEAD_DIM = 64
SSM_STATE = 128
SSM_GROUPS = 4
CONV_K = 4
SB_HEAD_DIM = 64
NEG_BIG = -1e30
VMEM_LIMIT = 56 * 1024 * 1024

ROW_TILE = 512
ATTN_TILE = 256
ATTN_LANE_TILES = 2
LOG2E = 1.4426950408889634
EXP_UNDERFLOW = -105.0


def _sigmoid(x):
    return 1.0 / (1.0 + jnp.exp(-x))


def _softplus(x):
    return jnp.maximum(x, 0.0) + jnp.log(1.0 + jnp.exp(-jnp.abs(x)))


def _rms_scale(x):
    return lax.rsqrt(jnp.mean(x * x, axis=-1, keepdims=True) + NORM_EPS)


def _dot(a, b):
    return jnp.dot(a, b, preferred_element_type=F32)


def _dot_nt(a, b):
    return lax.dot_general(a, b, (((1,), (1,)), ((), ())), preferred_element_type=F32)


def _dot_tn(a, b):
    return lax.dot_general(a, b, (((0,), (0,)), ((), ())), preferred_element_type=F32)


def _full(shape):
    return pl.BlockSpec(shape, lambda *_: (0,) * len(shape))


def _params(semantics):
    return pltpu.CompilerParams(dimension_semantics=semantics, vmem_limit_bytes=VMEM_LIMIT)


def _mamba_in_proj_kernel(x_ref, g_ref, wz_ref, wx_ref, wdt_ref, z_ref, xbc_ref, dt_ref):
    x = x_ref[...]
    u = (x * _rms_scale(x) * g_ref[...]).astype(BF16)
    z_ref[...] = _dot(u, wz_ref[...]).astype(z_ref.dtype)
    xbc_ref[...] = _dot(u, wx_ref[...]).astype(xbc_ref.dtype)
    dt_ref[...] = _dot(u, wdt_ref[...])


def _mamba_in_proj(x2, g, wz, wx, wdt):
    t, d = x2.shape
    row = lambda n: pl.BlockSpec((ROW_TILE, n), lambda i: (i, 0))
    return pl.pallas_call(
        _mamba_in_proj_kernel,
        grid=(t // ROW_TILE,),
        in_specs=[row(d), _full(g.shape), _full(wz.shape), _full(wx.shape), _full(wdt.shape)],
        out_specs=[row(wz.shape[1]), row(wx.shape[1]), row(wdt.shape[1])],
        out_shape=[jax.ShapeDtypeStruct((t, wz.shape[1]), BF16),
                   jax.ShapeDtypeStruct((t, wx.shape[1]), BF16),
                   jax.ShapeDtypeStruct((t, wdt.shape[1]), F32)],
        compiler_params=_params(("parallel",)),
        name="mamba_in_proj",
    )(x2, g, wz, wx, wdt)


def _mamba_ssd_kernel(z_ref, xbc_ref, dt_ref, cw_ref, cb_ref, dtb_ref, alog_ref, dskip_ref, yg_ref,
                      expand_ref, tril_ref, shift_ref, y_ref, ext_ref, state_ref, *, d_inner,
                      heads_per_group):
    L = SSD_CHUNK
    gn = SSM_GROUPS * SSM_STATE
    gw = heads_per_group * SSM_HEAD_DIM

    @pl.when(pl.program_id(1) == 0)
    def _():
        ext_ref[0:L, :] = jnp.zeros((L, ext_ref.shape[1]), ext_ref.dtype)
        state_ref[...] = jnp.zeros(state_ref.shape, F32)

    cur = xbc_ref[0]
    ext_ref[L:2 * L, :] = cur
    conv = cb_ref[...] + cw_ref[CONV_K - 1:CONV_K, :] * cur.astype(F32)
    for j in range(CONV_K - 1):
        conv = conv + cw_ref[j:j + 1, :] * _dot(shift_ref[j], ext_ref[...])
    ext_ref[0:L, :] = cur
    xbc = conv * _sigmoid(conv)
    x = xbc[:, :d_inner]
    bm = xbc[:, d_inner:d_inner + gn].astype(BF16)
    cm = xbc[:, d_inner + gn:].astype(BF16)

    dt = _softplus(dt_ref[0] + dtb_ref[...])
    a = dt * (-jnp.exp(alog_ref[...]))
    acum = _cumsum_time(tril_ref[...], a)
    acum_t = acum.T
    a_last = acum[L - 1:L, :]
    w_end = dt * jnp.exp(a_last - acum)
    stacked = jnp.concatenate(
        [dt, w_end, jnp.exp(acum), jnp.broadcast_to(jnp.exp(a_last), (8, LANES))], axis=0)
    hi = stacked.astype(BF16)
    lo = (stacked - hi.astype(F32)).astype(BF16)
    ex = _dot(jnp.concatenate([hi, lo], axis=1), expand_ref[...])
    dt_x, w_x, e_x, dec_x = ex[0:L], ex[L:2 * L], ex[2 * L:3 * L], ex[3 * L:3 * L + 1]

    xdt = (x * dt_x).astype(BF16)
    xw = (x * w_x).astype(BF16)

    row = lax.broadcasted_iota(jnp.int32, (L, L), 0)
    col = lax.broadcasted_iota(jnp.int32, (L, L), 1)
    causal = row >= col
    low_half = col < SSM_HEAD_DIM

    for g in range(SSM_GROUPS):
        bg = bm[:, g * SSM_STATE:(g + 1) * SSM_STATE]
        cg = cm[:, g * SSM_STATE:(g + 1) * SSM_STATE]
        gcols = slice(g * gw, (g + 1) * gw)
        cb = _dot_nt(cg, bg)
        state = state_ref[g]
        y_inter = _dot(cg, state.astype(BF16)) * e_x[:, gcols]
        tiles = []
        for hp in range(heads_per_group // 2):
            h0 = g * heads_per_group + 2 * hp
            cols = slice(h0 * SSM_HEAD_DIM, (h0 + 2) * SSM_HEAD_DIM)
            xd = xdt[:, cols]
            parts = []
            for h in (h0, h0 + 1):
                seg = acum[:, h:h + 1] - acum_t[h:h + 1, :]
                decay = jnp.exp(jnp.where(causal, seg, NEG_BIG))
                parts.append(_dot((cb * decay).astype(BF16), xd))
            y = jnp.where(low_half, parts[0], parts[1])
            y = y + y_inter[:, hp * LANES:(hp + 1) * LANES] + dskip_ref[:, cols] * x[:, cols]
            zc = z_ref[0, :, cols].astype(F32)
            tiles.append(y * (zc * _sigmoid(zc)))
        state_ref[g] = state * dec_x[:, gcols] + _dot_tn(bg, xw[:, gcols])
        ss = tiles[0] * tiles[0]
        for tl in tiles[1:]:
            ss = ss + tl * tl
        scale = lax.rsqrt(jnp.sum(ss, axis=-1, keepdims=True) * (1.0 / gw) + NORM_EPS)
        for hp, tl in enumerate(tiles):
            c0 = g * gw + hp * LANES
            y_ref[0, :, c0:c0 + LANES] = (tl * scale * yg_ref[:, c0:c0 + LANES]).astype(y_ref.dtype)


def _cumsum_time(tril, a):
    acc = None
    r = a
    for _ in range(3):
        hi = r.astype(BF16)
        d = _dot(tril, hi)
        acc = d if acc is None else acc + d
        r = r - hi.astype(F32)
    return acc


def _mamba_ssd(z, xbc, dt, conv_w, conv_b, dt_bias, a_log, d_skip, y_g, *, n_heads):
    b, s, d_inner = z.shape
    conv_dim = xbc.shape[-1]
    hpg = n_heads // SSM_GROUPS
    gw = hpg * SSM_HEAD_DIM
    L = SSD_CHUNK

    pad = LANES - n_heads
    dtb = jnp.pad(dt_bias.astype(F32), (0, pad)).reshape(1, LANES)
    alog = jnp.pad(a_log.astype(F32), (0, pad)).reshape(1, LANES)
    dskip = jnp.repeat(d_skip.astype(F32), SSM_HEAD_DIM).reshape(1, d_inner)
    expand = (jnp.arange(LANES)[:, None] == (jnp.arange(d_inner)[None, :] // SSM_HEAD_DIM)).astype(BF16)
    expand = jnp.concatenate([expand, expand], axis=0)
    tril = (jnp.arange(L)[:, None] >= jnp.arange(L)[None, :]).astype(BF16)
    t_idx = jnp.arange(L)[:, None]
    shift = jnp.stack([(jnp.arange(2 * L)[None, :] == L + t_idx - (CONV_K - 1 - j)).astype(BF16)
                       for j in range(CONV_K - 1)])

    blk = lambda n: pl.BlockSpec((1, L, n), lambda i, c: (i, c, 0))
    kern = functools.partial(_mamba_ssd_kernel, d_inner=d_inner, heads_per_group=hpg)
    return pl.pallas_call(
        kern,
        grid=(b, s // L),
        in_specs=[blk(d_inner), blk(conv_dim), blk(LANES),
                  _full(conv_w.shape), _full((1, conv_dim)), _full((1, LANES)), _full((1, LANES)),
                  _full((1, d_inner)), _full((1, d_inner)), _full(expand.shape), _full(tril.shape),
                  _full(shift.shape)],
        out_specs=blk(d_inner),
        out_shape=jax.ShapeDtypeStruct((b, s, d_inner), BF16),
        scratch_shapes=[pltpu.VMEM((2 * L, conv_dim), BF16),
                        pltpu.VMEM((SSM_GROUPS, SSM_STATE, gw), F32)],
        compiler_params=_params(("parallel", "arbitrary")),
        name="mamba_ssd",
    )(z, xbc, dt, conv_w.astype(F32), conv_b.astype(F32).reshape(1, conv_dim), dtb, alog, dskip,
      y_g.astype(F32).reshape(1, d_inner), expand, tril, shift)


def _layer_tail_kernel(y_ref, h_ref, p_ref, wo_ref, g_ref, wg_ref, wp_ref, o_ref):
    h1 = h_ref[...] + _dot(y_ref[...], wo_ref[...])
    u = (h1 * _rms_scale(h1) * g_ref[...]).astype(BF16)
    gate = _sigmoid(_dot(u, wg_ref[...]))
    pe = _dot(p_ref[...].astype(BF16), wp_ref[...])
    o_ref[...] = h1 + pe * gate


def _layer_tail(y2, h2, p2, w_out, ple_g, w_gate, w_proj):
    t, d = h2.shape
    row = lambda n: pl.BlockSpec((ROW_TILE, n), lambda i: (i, 0))
    return pl.pallas_call(
        _layer_tail_kernel,
        grid=(t // ROW_TILE,),
        in_specs=[row(y2.shape[1]), row(d), row(p2.shape[1]), _full(w_out.shape), _full((1, d)),
                  _full(w_gate.shape), _full(w_proj.shape)],
        out_specs=row(d),
        out_shape=jax.ShapeDtypeStruct((t, d), F32),
        compiler_params=_params(("parallel",)),
        name="layer_tail",
    )(y2, h2, p2, w_out.astype(BF16), ple_g.astype(F32).reshape(1, d), w_gate.astype(BF16),
      w_proj.astype(BF16))


def _head_norm(x, gain):
    rows, width = x.shape
    low = lax.broadcasted_iota(jnp.int32, (rows, LANES), 1) < SB_HEAD_DIM
    out = []
    for j in range(width // LANES):
        blk = x[:, j * LANES:(j + 1) * LANES]
        sq = blk * blk
        s_lo = jnp.sum(jnp.where(low, sq, 0.0), axis=-1, keepdims=True)
        s_hi = jnp.sum(jnp.where(low, 0.0, sq), axis=-1, keepdims=True)
        ms = jnp.where(low, s_lo, s_hi) * (1.0 / SB_HEAD_DIM)
        out.append(blk * lax.rsqrt(ms + NORM_EPS) * gain[:, j * LANES:(j + 1) * LANES])
    return out


def _attn_in_proj_kernel(h_ref, gkv_ref, gq_ref, wk_ref, wv_ref, wq_ref, wgate_ref, kg_ref, qg_ref,
                         q_ref, k_ref, v_ref, gate_ref):
    h = h_ref[...]
    hn = h * _rms_scale(h)
    ukv = (hn * gkv_ref[...]).astype(BF16)
    uq = (hn * gq_ref[...]).astype(BF16)
    v_ref[...] = _dot(ukv, wv_ref[...]).astype(v_ref.dtype)
    gate_ref[...] = _dot(uq, wgate_ref[...]).astype(gate_ref.dtype)
    for j, blk in enumerate(_head_norm(_dot(ukv, wk_ref[...]), kg_ref[...])):
        k_ref[:, j * LANES:(j + 1) * LANES] = blk.astype(k_ref.dtype)
    for j, blk in enumerate(_head_norm(_dot(uq, wq_ref[...]), qg_ref[...])):
        q_ref[:, j * LANES:(j + 1) * LANES] = blk.astype(q_ref.dtype)


def _attn_in_proj(h2, kv_norm, s_norm, w_kv, s_in, k_norm, q_norm):
    t, d = h2.shape
    width = w_kv.shape[1] // 2
    n_heads = width // SB_HEAD_DIM
    vec = lambda v: v.astype(F32).reshape(1, -1)
    kg = vec(jnp.tile(k_norm, n_heads))
    qg = vec(jnp.tile(q_norm * (1.0 / math.sqrt(SB_HEAD_DIM)), n_heads))
    wk, wv = w_kv[:, :width].astype(BF16), w_kv[:, width:].astype(BF16)
    wq, wgate = s_in[:, :width].astype(BF16), s_in[:, width:].astype(BF16)
    row = lambda n: pl.BlockSpec((ROW_TILE, n), lambda i: (i, 0))
    out = jax.ShapeDtypeStruct((t, width), BF16)
    return pl.pallas_call(
        _attn_in_proj_kernel,
        grid=(t // ROW_TILE,),
        in_specs=[row(d), _full((1, d)), _full((1, d)), _full(wk.shape), _full(wv.shape),
                  _full(wq.shape), _full(wgate.shape), _full((1, width)), _full((1, width))],
        out_specs=[row(width)] * 4,
        out_shape=[out] * 4,
        compiler_params=_params(("parallel",)),
        name="attn_in_proj",
    )(h2, vec(kv_norm), vec(s_norm), wk, wv, wq, wgate, kg, qg)


def _sb_attention_kernel(q_ref, k_ref, v_ref, gate_ref, tri_ref, o_ref, acc_ref):
    tq = ATTN_TILE
    n_heads = 2 * ATTN_LANE_TILES
    qi = pl.program_id(2)
    low = lax.broadcasted_iota(jnp.int32, (tq, LANES), 1) < SB_HEAD_DIM
    q_heads = []
    for lt in range(ATTN_LANE_TILES):
        q2 = q_ref[0, :, lt * LANES:(lt + 1) * LANES]
        zero = jnp.zeros_like(q2)
        q_heads += [jnp.where(low, q2, zero), jnp.where(low, zero, q2)]
    tri = tri_ref[...]
    row = lax.broadcasted_iota(jnp.int32, (tq, tq), 0)
    col = lax.broadcasted_iota(jnp.int32, (tq, tq), 1)
    strict = col < row

    lanes = [slice((h // 2) * LANES, (h // 2 + 1) * LANES) for h in range(n_heads)]

    def tile_step(tiles, carry, first):
        rows = [pl.ds(pl.multiple_of(kt * tq, tq), tq) for kt, _, _ in tiles]
        units = [(j, h) for j in range(len(tiles)) for h in range(n_heads)]
        diag = {u: tiles[u[0]][1] for u in units}
        z = {u: _dot_nt(q_heads[u[1]], k_ref[0, rows[u[0]], lanes[u[1]]]) for u in units}
        sp = {u: jnp.maximum(z[u], 0.0) + jnp.log(1.0 + jnp.exp2(jnp.abs(z[u]) * -LOG2E)) for u in units}
        sp = {u: jnp.where(strict, sp[u], 0.0) if diag[u] else sp[u] for u in units}
        total = {u: jnp.sum(sp[u], axis=-1, keepdims=True) for u in units}
        later = {u: _dot(sp[u].astype(BF16), tri) for u in units}
        logw = {u: (z[u] - sp[u]) - later[u] for u in units}
        logw = {u: jnp.where(strict, logw[u], NEG_BIG) if diag[u] else logw[u] for u in units}
        pv = {u: _dot(jnp.exp(logw[u]).astype(BF16), v_ref[0, rows[u[0]], lanes[u[1]]]) for u in units}
        carry = list(carry)
        for h in range(n_heads):
            contrib = None
            for j, (_, _, valid) in enumerate(tiles):
                c = jnp.exp(carry[h]) * pv[(j, h)]
                t = total[(j, h)]
                if valid is not None:
                    c = jnp.where(valid, c, 0.0)
                    t = jnp.where(valid, t, 0.0)
                contrib = c if contrib is None else contrib + c
                carry[h] = carry[h] - t
            if first:
                acc_ref[h] = contrib
            else:
                acc_ref[h] += contrib
        return tuple(carry), jnp.max(functools.reduce(jnp.maximum, carry))

    def keep_going(state):
        i, live, _ = state
        return jnp.logical_and(i < qi, live > EXP_UNDERFLOW)

    def walk(state):
        i, _, carry = state
        carry, live = tile_step([(qi - 1 - i, False, None)], carry, False)
        return i + 1, live, carry

    zeros = jnp.zeros((tq, 1), F32)
    carry, live = tile_step([(qi, True, None), (jnp.maximum(qi - 1, 0), False, qi > 0)],
                            (zeros,) * n_heads, True)
    lax.while_loop(keep_going, walk, (jnp.int32(1), live, carry))

    for lt in range(ATTN_LANE_TILES):
        lanes = slice(lt * LANES, (lt + 1) * LANES)
        o = jnp.where(low, acc_ref[2 * lt], acc_ref[2 * lt + 1])
        gt = gate_ref[0, :, lanes].astype(F32)
        o_ref[0, :, lanes] = (o * (gt * _sigmoid(gt))).astype(o_ref.dtype)


def _sb_attention(q, k, v, gate):
    b, s, width = q.shape
    tq = ATTN_TILE
    gw = ATTN_LANE_TILES * LANES
    tri = (jnp.arange(tq)[:, None] > jnp.arange(tq)[None, :]).astype(BF16)
    qblk = pl.BlockSpec((1, tq, gw), lambda i, hg, t: (i, t, hg))
    kvblk = pl.BlockSpec((1, s, gw), lambda i, hg, t: (i, 0, hg))
    return pl.pallas_call(
        _sb_attention_kernel,
        grid=(b, width // gw, s // tq),
        in_specs=[qblk, kvblk, kvblk, qblk, _full(tri.shape)],
        out_specs=qblk,
        out_shape=jax.ShapeDtypeStruct((b, s, width), BF16),
        scratch_shapes=[pltpu.VMEM((2 * ATTN_LANE_TILES, tq, LANES), F32)],
        compiler_params=_params(("parallel", "parallel", "arbitrary")),
        name="sb_attention",
    )(q, k, v, gate, tri)


def kernel(x, p, m_norm, m_in, m_conv_w, m_conv_b, m_dt_bias, m_A_log, m_D, m_ynorm, m_out, kv_norm, w_kv,
           k_norm, s_norm, s_in, q_norm, s_out, ple_norm, ple_gate, ple_proj):
    b, s, d = x.shape
    t = b * s
    n_a = m_norm.shape[0]
    depth = p.shape[0]
    n_heads = m_dt_bias.shape[1]
    d_inner = n_heads * SSM_HEAD_DIM
    conv_dim = m_conv_w.shape[2]

    h = x.reshape(t, d)
    kv = None
    for i in range(depth):
        if i < n_a:
            w_in = m_in[i]
            wz = w_in[:, :d_inner].astype(BF16)
            wx = w_in[:, d_inner:d_inner + conv_dim].astype(BF16)
            wdt = jnp.pad(w_in[:, d_inner + conv_dim:], ((0, 0), (0, LANES - n_heads))).astype(BF16)
            z, xbc, dt = _mamba_in_proj(h, m_norm[i].astype(F32).reshape(1, d), wz, wx, wdt)
            y = _mamba_ssd(z.reshape(b, s, -1), xbc.reshape(b, s, -1), dt.reshape(b, s, -1),
                           m_conv_w[i], m_conv_b[i], m_dt_bias[i], m_A_log[i], m_D[i], m_ynorm[i],
                           n_heads=n_heads)
            y = y.reshape(t, d_inner)
            w_out = m_out[i]
        else:
            j = i - n_a
            q, k, v, gate = _attn_in_proj(h, kv_norm, s_norm[j], w_kv, s_in[j], k_norm, q_norm[j])
            if kv is None:
                kv = (k, v)
            k, v = kv
            width = q.shape[1]
            y = _sb_attention(q.reshape(b, s, width), k.reshape(b, s, width), v.reshape(b, s, width),
                              gate.reshape(b, s, width)).reshape(t, width)
            w_out = s_out[j]
        h = _layer_tail(y, h, p[i].reshape(t, -1), w_out, ple_norm[i], ple_gate[i], ple_proj[i])
    return h.reshape(b, s, d)
```

```python
import functools
import math

import jax
import jax.numpy as jnp
from jax import lax
from jax.experimental import pallas as pl
from jax.experimental.pallas import tpu as pltpu

F32 = jnp.float32
BF16 = jnp.bfloat16

NORM_EPS = 1e-6
LANES = 128
SSD_CHUNK = 128
SSM_HEAD_DIM = 64
SSM_STATE = 128
SSM_GROUPS = 4
CONV_K = 4
SB_HEAD_DIM = 64
NEG_BIG = -1e30
VMEM_LIMIT = 56 * 1024 * 1024

ROW_TILE = 512
ATTN_TILE = 256
ATTN_LANE_TILES = 4
LOG2E = 1.4426950408889634
EXP_UNDERFLOW = -105.0


def _sigmoid(x):
    return 1.0 / (1.0 + jnp.exp2(x * -LOG2E))


def _softplus(x):
    return jnp.maximum(x, 0.0) + jnp.log(1.0 + jnp.exp(-jnp.abs(x)))


def _rms_scale(x):
    return lax.rsqrt(jnp.mean(x * x, axis=-1, keepdims=True) + NORM_EPS)


def _dot(a, b):
    return jnp.dot(a, b, preferred_element_type=F32)


def _dot_nt(a, b):
    return lax.dot_general(a, b, (((1,), (1,)), ((), ())), preferred_element_type=F32)


def _dot_tn(a, b):
    return lax.dot_general(a, b, (((0,), (0,)), ((), ())), preferred_element_type=F32)


def _full(shape):
    return pl.BlockSpec(shape, lambda *_: (0,) * len(shape))


def _params(semantics):
    return pltpu.CompilerParams(dimension_semantics=semantics, vmem_limit_bytes=VMEM_LIMIT)


def _mamba_in_proj_kernel(x_ref, g_ref, wz_ref, wx_ref, wdt_ref, z_ref, xbc_ref, dt_ref):
    x = x_ref[...]
    u = (x * _rms_scale(x) * g_ref[...]).astype(BF16)
    z_ref[...] = _dot(u, wz_ref[...]).astype(z_ref.dtype)
    xbc_ref[...] = _dot(u, wx_ref[...]).astype(xbc_ref.dtype)
    dt_ref[...] = _dot(u, wdt_ref[...])


def _mamba_in_proj(x2, g, wz, wx, wdt):
    t, d = x2.shape
    row = lambda n: pl.BlockSpec((ROW_TILE, n), lambda i: (i, 0))
    return pl.pallas_call(
        _mamba_in_proj_kernel,
        grid=(t // ROW_TILE,),
        in_specs=[row(d), _full(g.shape), _full(wz.shape), _full(wx.shape), _full(wdt.shape)],
        out_specs=[row(wz.shape[1]), row(wx.shape[1]), row(wdt.shape[1])],
        out_shape=[jax.ShapeDtypeStruct((t, wz.shape[1]), BF16),
                   jax.ShapeDtypeStruct((t, wx.shape[1]), BF16),
                   jax.ShapeDtypeStruct((t, wdt.shape[1]), F32)],
        compiler_params=_params(("parallel",)),
        name="mamba_in_proj",
    )(x2, g, wz, wx, wdt)


def _mamba_ssd_kernel(z_ref, xbc_ref, dt_ref, cw_ref, cb_ref, dtb_ref, alog_ref, dskip_ref, yg_ref,
                      expand_ref, tril_ref, shift_ref, y_ref, ext_ref, state_ref, *, d_inner,
                      heads_per_group):
    L = SSD_CHUNK
    gn = SSM_GROUPS * SSM_STATE
    gw = heads_per_group * SSM_HEAD_DIM

    @pl.when(pl.program_id(1) == 0)
    def _():
        ext_ref[0:L, :] = jnp.zeros((L, ext_ref.shape[1]), ext_ref.dtype)
        state_ref[...] = jnp.zeros(state_ref.shape, F32)

    cur = xbc_ref[0]
    ext_ref[L:2 * L, :] = cur
    conv = cb_ref[...] + cw_ref[CONV_K - 1:CONV_K, :] * cur.astype(F32)
    for j in range(CONV_K - 1):
        conv = conv + cw_ref[j:j + 1, :] * _dot(shift_ref[j], ext_ref[...])
    ext_ref[0:L, :] = cur
    xbc = conv * _sigmoid(conv)
    x = xbc[:, :d_inner]
    bm = xbc[:, d_inner:d_inner + gn].astype(BF16)
    cm = xbc[:, d_inner + gn:].astype(BF16)

    dt = _softplus(dt_ref[0] + dtb_ref[...])
    a = dt * (-jnp.exp(alog_ref[...]))
    acum = _cumsum_time(tril_ref[...], a)
    acum_t = acum.T
    a_last = acum[L - 1:L, :]
    w_end = dt * jnp.exp(a_last - acum)
    stacked = jnp.concatenate(
        [dt, w_end, jnp.exp(acum), jnp.broadcast_to(jnp.exp(a_last), (8, LANES))], axis=0)
    hi = stacked.astype(BF16)
    lo = (stacked - hi.astype(F32)).astype(BF16)
    ex = _dot(jnp.concatenate([hi, lo], axis=1), expand_ref[...])
    dt_x, w_x, e_x, dec_x = ex[0:L], ex[L:2 * L], ex[2 * L:3 * L], ex[3 * L:3 * L + 1]

    xdt = (x * dt_x).astype(BF16)
    xw = (x * w_x).astype(BF16)

    row = lax.broadcasted_iota(jnp.int32, (L, L), 0)
    col = lax.broadcasted_iota(jnp.int32, (L, L), 1)
    causal = row >= col
    low_half = col < SSM_HEAD_DIM

    for g in range(SSM_GROUPS):
        bg = bm[:, g * SSM_STATE:(g + 1) * SSM_STATE]
        cg = cm[:, g * SSM_STATE:(g + 1) * SSM_STATE]
        gcols = slice(g * gw, (g + 1) * gw)
        cb = _dot_nt(cg, bg)
        state = state_ref[g]
        y_inter = _dot(cg, state.astype(BF16)) * e_x[:, gcols]
        tiles = []
        for hp in range(heads_per_group // 2):
            h0 = g * heads_per_group + 2 * hp
            cols = slice(h0 * SSM_HEAD_DIM, (h0 + 2) * SSM_HEAD_DIM)
            xd = xdt[:, cols]
            parts = []
            for h in (h0, h0 + 1):
                seg = acum[:, h:h + 1] - acum_t[h:h + 1, :]
                decay = jnp.exp(jnp.where(causal, seg, NEG_BIG))
                parts.append(_dot((cb * decay).astype(BF16), xd))
            y = jnp.where(low_half, parts[0], parts[1])
            y = y + y_inter[:, hp * LANES:(hp + 1) * LANES] + dskip_ref[:, cols] * x[:, cols]
            zc = z_ref[0, :, cols].astype(F32)
            tiles.append(y * (zc * _sigmoid(zc)))
        state_ref[g] = state * dec_x[:, gcols] + _dot_tn(bg, xw[:, gcols])
        ss = tiles[0] * tiles[0]
        for tl in tiles[1:]:
            ss = ss + tl * tl
        scale = lax.rsqrt(jnp.sum(ss, axis=-1, keepdims=True) * (1.0 / gw) + NORM_EPS)
        for hp, tl in enumerate(tiles):
            c0 = g * gw + hp * LANES
            y_ref[0, :, c0:c0 + LANES] = (tl * scale * yg_ref[:, c0:c0 + LANES]).astype(y_ref.dtype)


def _cumsum_time(tril, a):
    acc = None
    r = a
    for _ in range(3):
        hi = r.astype(BF16)
        d = _dot(tril, hi)
        acc = d if acc is None else acc + d
        r = r - hi.astype(F32)
    return acc


def _mamba_ssd(z, xbc, dt, conv_w, conv_b, dt_bias, a_log, d_skip, y_g, *, n_heads):
    b, s, d_inner = z.shape
    conv_dim = xbc.shape[-1]
    hpg = n_heads // SSM_GROUPS
    gw = hpg * SSM_HEAD_DIM
    L = SSD_CHUNK

    pad = LANES - n_heads
    dtb = jnp.pad(dt_bias.astype(F32), (0, pad)).reshape(1, LANES)
    alog = jnp.pad(a_log.astype(F32), (0, pad)).reshape(1, LANES)
    dskip = jnp.repeat(d_skip.astype(F32), SSM_HEAD_DIM).reshape(1, d_inner)
    expand = (jnp.arange(LANES)[:, None] == (jnp.arange(d_inner)[None, :] // SSM_HEAD_DIM)).astype(BF16)
    expand = jnp.concatenate([expand, expand], axis=0)
    tril = (jnp.arange(L)[:, None] >= jnp.arange(L)[None, :]).astype(BF16)
    t_idx = jnp.arange(L)[:, None]
    shift = jnp.stack([(jnp.arange(2 * L)[None, :] == L + t_idx - (CONV_K - 1 - j)).astype(BF16)
                       for j in range(CONV_K - 1)])

    blk = lambda n: pl.BlockSpec((1, L, n), lambda i, c: (i, c, 0))
    kern = functools.partial(_mamba_ssd_kernel, d_inner=d_inner, heads_per_group=hpg)
    return pl.pallas_call(
        kern,
        grid=(b, s // L),
        in_specs=[blk(d_inner), blk(conv_dim), blk(LANES),
                  _full(conv_w.shape), _full((1, conv_dim)), _full((1, LANES)), _full((1, LANES)),
                  _full((1, d_inner)), _full((1, d_inner)), _full(expand.shape), _full(tril.shape),
                  _full(shift.shape)],
        out_specs=blk(d_inner),
        out_shape=jax.ShapeDtypeStruct((b, s, d_inner), BF16),
        scratch_shapes=[pltpu.VMEM((2 * L, conv_dim), BF16),
                        pltpu.VMEM((SSM_GROUPS, SSM_STATE, gw), F32)],
        compiler_params=_params(("parallel", "arbitrary")),
        name="mamba_ssd",
    )(z, xbc, dt, conv_w.astype(F32), conv_b.astype(F32).reshape(1, conv_dim), dtb, alog, dskip,
      y_g.astype(F32).reshape(1, d_inner), expand, tril, shift)


def _layer_tail_kernel(y_ref, h_ref, p_ref, wo_ref, g_ref, wg_ref, wp_ref, o_ref):
    h1 = h_ref[...] + _dot(y_ref[...], wo_ref[...])
    u = (h1 * _rms_scale(h1) * g_ref[...]).astype(BF16)
    gate = _sigmoid(_dot(u, wg_ref[...]))
    pe = _dot(p_ref[...].astype(BF16), wp_ref[...])
    o_ref[...] = h1 + pe * gate


def _layer_tail(y2, h2, p2, w_out, ple_g, w_gate, w_proj):
    t, d = h2.shape
    row = lambda n: pl.BlockSpec((ROW_TILE, n), lambda i: (i, 0))
    return pl.pallas_call(
        _layer_tail_kernel,
        grid=(t // ROW_TILE,),
        in_specs=[row(y2.shape[1]), row(d), row(p2.shape[1]), _full(w_out.shape), _full((1, d)),
                  _full(w_gate.shape), _full(w_proj.shape)],
        out_specs=row(d),
        out_shape=jax.ShapeDtypeStruct((t, d), F32),
        compiler_params=_params(("parallel",)),
        name="layer_tail",
    )(y2, h2, p2, w_out.astype(BF16), ple_g.astype(F32).reshape(1, d), w_gate.astype(BF16),
      w_proj.astype(BF16))


def _head_norm(x, gain):
    rows, width = x.shape
    low = lax.broadcasted_iota(jnp.int32, (rows, LANES), 1) < SB_HEAD_DIM
    out = []
    for j in range(width // LANES):
        blk = x[:, j * LANES:(j + 1) * LANES]
        sq = blk * blk
        s_lo = jnp.sum(jnp.where(low, sq, 0.0), axis=-1, keepdims=True)
        s_hi = jnp.sum(jnp.where(low, 0.0, sq), axis=-1, keepdims=True)
        ms = jnp.where(low, s_lo, s_hi) * (1.0 / SB_HEAD_DIM)
        out.append(blk * lax.rsqrt(ms + NORM_EPS) * gain[:, j * LANES:(j + 1) * LANES])
    return out


def _attn_in_proj_kernel(h_ref, gkv_ref, gq_ref, wk_ref, wv_ref, wq_ref, wgate_ref, kg_ref, qg_ref,
                         q_ref, k_ref, v_ref, gate_ref):
    h = h_ref[...]
    hn = h * _rms_scale(h)
    ukv = (hn * gkv_ref[...]).astype(BF16)
    uq = (hn * gq_ref[...]).astype(BF16)
    v_ref[...] = _dot(ukv, wv_ref[...]).astype(v_ref.dtype)
    gate_ref[...] = _dot(uq, wgate_ref[...]).astype(gate_ref.dtype)
    for j, blk in enumerate(_head_norm(_dot(ukv, wk_ref[...]), kg_ref[...])):
        k_ref[:, j * LANES:(j + 1) * LANES] = blk.astype(k_ref.dtype)
    for j, blk in enumerate(_head_norm(_dot(uq, wq_ref[...]), qg_ref[...])):
        q_ref[:, j * LANES:(j + 1) * LANES] = blk.astype(q_ref.dtype)


def _attn_in_proj(h2, kv_norm, s_norm, w_kv, s_in, k_norm, q_norm):
    t, d = h2.shape
    width = w_kv.shape[1] // 2
    n_heads = width // SB_HEAD_DIM
    vec = lambda v: v.astype(F32).reshape(1, -1)
    kg = vec(jnp.tile(k_norm, n_heads))
    qg = vec(jnp.tile(q_norm * (1.0 / math.sqrt(SB_HEAD_DIM)), n_heads))
    wk, wv = w_kv[:, :width].astype(BF16), w_kv[:, width:].astype(BF16)
    wq, wgate = s_in[:, :width].astype(BF16), s_in[:, width:].astype(BF16)
    row = lambda n: pl.BlockSpec((ROW_TILE, n), lambda i: (i, 0))
    out = jax.ShapeDtypeStruct((t, width), BF16)
    return pl.pallas_call(
        _attn_in_proj_kernel,
        grid=(t // ROW_TILE,),
        in_specs=[row(d), _full((1, d)), _full((1, d)), _full(wk.shape), _full(wv.shape),
                  _full(wq.shape), _full(wgate.shape), _full((1, width)), _full((1, width))],
        out_specs=[row(width)] * 4,
        out_shape=[out] * 4,
        compiler_params=_params(("parallel",)),
        name="attn_in_proj",
    )(h2, vec(kv_norm), vec(s_norm), wk, wv, wq, wgate, kg, qg)


def _sb_attention_kernel(q_ref, k_ref, v_ref, gate_ref, tri_ref, o_ref, acc_ref):
    tq = ATTN_TILE
    n_heads = 2 * ATTN_LANE_TILES
    qi = pl.program_id(2)
    low = lax.broadcasted_iota(jnp.int32, (tq, LANES), 1) < SB_HEAD_DIM
    q_heads = []
    for lt in range(ATTN_LANE_TILES):
        q2 = q_ref[0, :, lt * LANES:(lt + 1) * LANES]
        zero = jnp.zeros_like(q2)
        q_heads += [jnp.where(low, q2, zero), jnp.where(low, zero, q2)]
    tri = tri_ref[...]
    row = lax.broadcasted_iota(jnp.int32, (tq, tq), 0)
    col = lax.broadcasted_iota(jnp.int32, (tq, tq), 1)
    strict = col < row

    lanes = [slice((h // 2) * LANES, (h // 2 + 1) * LANES) for h in range(n_heads)]

    def tile_step(tiles, carry, first):
        rows = [pl.ds(pl.multiple_of(kt * tq, tq), tq) for kt, _, _ in tiles]
        units = [(j, h) for j in range(len(tiles)) for h in range(n_heads)]
        diag = {u: tiles[u[0]][1] for u in units}
        z = {u: _dot_nt(q_heads[u[1]], k_ref[0, rows[u[0]], lanes[u[1]]]) for u in units}
        sp = {u: jnp.maximum(z[u], 0.0) + jnp.log(1.0 + jnp.exp2(jnp.abs(z[u]) * -LOG2E)) for u in units}
        sp = {u: jnp.where(strict, sp[u], 0.0) if diag[u] else sp[u] for u in units}
        total = {u: jnp.sum(sp[u], axis=-1, keepdims=True) for u in units}
        later = {u: _dot(sp[u].astype(BF16), tri) for u in units}
        logw = {u: (z[u] - sp[u]) - later[u] for u in units}
        logw = {u: jnp.where(strict, logw[u], NEG_BIG) if diag[u] else logw[u] for u in units}
        pv = {u: _dot(jnp.exp(logw[u]).astype(BF16), v_ref[0, rows[u[0]], lanes[u[1]]]) for u in units}
        carry = list(carry)
        for h in range(n_heads):
            contrib = None
            for j, (_, _, valid) in enumerate(tiles):
                c = jnp.exp(carry[h]) * pv[(j, h)]
                t = total[(j, h)]
                if valid is not None:
                    c = jnp.where(valid, c, 0.0)
                    t = jnp.where(valid, t, 0.0)
                contrib = c if contrib is None else contrib + c
                carry[h] = carry[h] - t
            if first:
                acc_ref[h] = contrib
            else:
                acc_ref[h] += contrib
        return tuple(carry), jnp.max(functools.reduce(jnp.maximum, carry))

    def keep_going(state):
        i, live, _ = state
        return jnp.logical_and(i < qi, live > EXP_UNDERFLOW)

    def walk(state):
        i, _, carry = state
        carry, live = tile_step([(qi - 1 - i, False, None)], carry, False)
        return i + 1, live, carry

    zeros = jnp.zeros((tq, 1), F32)
    carry, live = tile_step([(qi, True, None), (jnp.maximum(qi - 1, 0), False, qi > 0)],
                            (zeros,) * n_heads, True)
    lax.while_loop(keep_going, walk, (jnp.int32(1), live, carry))

    for lt in range(ATTN_LANE_TILES):
        lanes = slice(lt * LANES, (lt + 1) * LANES)
        o = jnp.where(low, acc_ref[2 * lt], acc_ref[2 * lt + 1])
        gt = gate_ref[0, :, lanes].astype(F32)
        o_ref[0, :, lanes] = (o * (gt * _sigmoid(gt))).astype(o_ref.dtype)


def _sb_attention(q, k, v, gate):
    b, s, width = q.shape
    tq = ATTN_TILE
    gw = ATTN_LANE_TILES * LANES
    tri = (jnp.arange(tq)[:, None] > jnp.arange(tq)[None, :]).astype(BF16)
    qblk = pl.BlockSpec((1, tq, gw), lambda i, hg, t: (i, t, hg))
    kvblk = pl.BlockSpec((1, s, gw), lambda i, hg, t: (i, 0, hg))
    return pl.pallas_call(
        _sb_attention_kernel,
        grid=(b, width // gw, s // tq),
        in_specs=[qblk, kvblk, kvblk, qblk, _full(tri.shape)],
        out_specs=qblk,
        out_shape=jax.ShapeDtypeStruct((b, s, width), BF16),
        scratch_shapes=[pltpu.VMEM((2 * ATTN_LANE_TILES, tq, LANES), F32)],
        compiler_params=_params(("parallel", "parallel", "arbitrary")),
        name="sb_attention",
    )(q, k, v, gate, tri)


def kernel(x, p, m_norm, m_in, m_conv_w, m_conv_b, m_dt_bias, m_A_log, m_D, m_ynorm, m_out, kv_norm, w_kv,
           k_norm, s_norm, s_in, q_norm, s_out, ple_norm, ple_gate, ple_proj):
    b, s, d = x.shape
    t = b * s
    n_a = m_norm.shape[0]
    depth = p.shape[0]
    n_heads = m_dt_bias.shape[1]
    d_inner = n_heads * SSM_HEAD_DIM
    conv_dim = m_conv_w.shape[2]

    h = x.reshape(t, d)
    kv = None
    for i in range(depth):
        if i < n_a:
            w_in = m_in[i]
            wz = w_in[:, :d_inner].astype(BF16)
            wx = w_in[:, d_inner:d_inner + conv_dim].astype(BF16)
            wdt = jnp.pad(w_in[:, d_inner + conv_dim:], ((0, 0), (0, LANES - n_heads))).astype(BF16)
            z, xbc, dt = _mamba_in_proj(h, m_norm[i].astype(F32).reshape(1, d), wz, wx, wdt)
            y = _mamba_ssd(z.reshape(b, s, -1), xbc.reshape(b, s, -1), dt.reshape(b, s, -1),
                           m_conv_w[i], m_conv_b[i], m_dt_bias[i], m_A_log[i], m_D[i], m_ynorm[i],
                           n_heads=n_heads)
            y = y.reshape(t, d_inner)
            w_out = m_out[i]
        else:
            j = i - n_a
            q, k, v, gate = _attn_in_proj(h, kv_norm, s_norm[j], w_kv, s_in[j], k_norm, q_norm[j])
            if kv is None:
                kv = (k, v)
            k, v = kv
            width = q.shape[1]
            y = _sb_attention(q.reshape(b, s, width), k.reshape(b, s, width), v.reshape(b, s, width),
                              gate.reshape(b, s, width)).reshape(t, width)
            w_out = s_out[j]
        h = _layer_tail(y, h, p[i].reshape(t, -1), w_out, ple_norm[i], ple_gate[i], ple_proj[i])
    return h.reshape(b, s, d)
```

```python
import functools
import math

import jax
import jax.numpy as jnp
from jax import lax
from jax.experimental import pallas as pl
from jax.experimental.pallas import tpu as pltpu

F32 = jnp.float32
BF16 = jnp.bfloat16

NORM_EPS = 1e-6
LANES = 128
SSD_CHUNK = 128
SSD_CHUNKS_PER_STEP = 4
SSM_HEAD_DIM = 64
SSM_STATE = 128
SSM_GROUPS = 4
CONV_K = 4
SB_HEAD_DIM = 64
NEG_BIG = -1e30
VMEM_LIMIT = 56 * 1024 * 1024

ROW_TILE = 512
ATTN_TILE = 256
ATTN_LANE_TILES = 4
LOG2E = 1.4426950408889634
EXP_UNDERFLOW = -105.0


def _sigmoid(x):
    return 1.0 / (1.0 + jnp.exp2(x * -LOG2E))


def _softplus(x):
    return jnp.maximum(x, 0.0) + jnp.log(1.0 + jnp.exp(-jnp.abs(x)))


def _rms_scale(x):
    return lax.rsqrt(jnp.mean(x * x, axis=-1, keepdims=True) + NORM_EPS)


def _dot(a, b):
    return jnp.dot(a, b, preferred_element_type=F32)


def _dot_nt(a, b):
    return lax.dot_general(a, b, (((1,), (1,)), ((), ())), preferred_element_type=F32)


def _dot_tn(a, b):
    return lax.dot_general(a, b, (((0,), (0,)), ((), ())), preferred_element_type=F32)


def _full(shape):
    return pl.BlockSpec(shape, lambda *_: (0,) * len(shape))


def _params(semantics):
    return pltpu.CompilerParams(dimension_semantics=semantics, vmem_limit_bytes=VMEM_LIMIT)


def _mamba_in_proj_kernel(x_ref, g_ref, wz_ref, wx_ref, wdt_ref, z_ref, xbc_ref, dt_ref):
    x = x_ref[...]
    u = (x * _rms_scale(x) * g_ref[...]).astype(BF16)
    z_ref[...] = _dot(u, wz_ref[...]).astype(z_ref.dtype)
    xbc_ref[...] = _dot(u, wx_ref[...]).astype(xbc_ref.dtype)
    dt_ref[...] = _dot(u, wdt_ref[...])


def _mamba_in_proj(x2, g, wz, wx, wdt):
    t, d = x2.shape
    row = lambda n: pl.BlockSpec((ROW_TILE, n), lambda i: (i, 0))
    return pl.pallas_call(
        _mamba_in_proj_kernel,
        grid=(t // ROW_TILE,),
        in_specs=[row(d), _full(g.shape), _full(wz.shape), _full(wx.shape), _full(wdt.shape)],
        out_specs=[row(wz.shape[1]), row(wx.shape[1]), row(wdt.shape[1])],
        out_shape=[jax.ShapeDtypeStruct((t, wz.shape[1]), BF16),
                   jax.ShapeDtypeStruct((t, wx.shape[1]), BF16),
                   jax.ShapeDtypeStruct((t, wdt.shape[1]), F32)],
        compiler_params=_params(("parallel",)),
        name="mamba_in_proj",
    )(x2, g, wz, wx, wdt)


def _mamba_ssd_kernel(z_ref, xbc_ref, dt_ref, cw_ref, cb_ref, dtb_ref, alog_ref, dskip_ref, yg_ref,
                      expand_ref, tril_ref, shift_ref, y_ref, ext_ref, state_ref, *, d_inner,
                      heads_per_group):
    L = SSD_CHUNK
    gn = SSM_GROUPS * SSM_STATE
    gw = heads_per_group * SSM_HEAD_DIM

    @pl.when(pl.program_id(1) == 0)
    def _():
        ext_ref[0:L, :] = jnp.zeros((L, ext_ref.shape[1]), ext_ref.dtype)
        state_ref[...] = jnp.zeros(state_ref.shape, F32)

    lax.fori_loop(0, SSD_CHUNKS_PER_STEP, functools.partial(
        _ssd_chunk, z_ref, xbc_ref, dt_ref, cw_ref, cb_ref, dtb_ref, alog_ref, dskip_ref, yg_ref, expand_ref,
        tril_ref, shift_ref, y_ref, ext_ref, state_ref, d_inner, heads_per_group), None)


def _ssd_chunk(z_ref, xbc_ref, dt_ref, cw_ref, cb_ref, dtb_ref, alog_ref, dskip_ref, yg_ref, expand_ref,
               tril_ref, shift_ref, y_ref, ext_ref, state_ref, d_inner, heads_per_group, c, _):
    L = SSD_CHUNK
    gn = SSM_GROUPS * SSM_STATE
    gw = heads_per_group * SSM_HEAD_DIM
    rows = pl.ds(pl.multiple_of(c * L, L), L)

    cur = xbc_ref[0, rows, :]
    ext_ref[L:2 * L, :] = cur
    conv = cb_ref[...] + cw_ref[CONV_K - 1:CONV_K, :] * cur.astype(F32)
    for j in range(CONV_K - 1):
        conv = conv + cw_ref[j:j + 1, :] * _dot(shift_ref[j], ext_ref[...])
    ext_ref[0:L, :] = cur
    xbc = conv * _sigmoid(conv)
    x = xbc[:, :d_inner]
    bm = xbc[:, d_inner:d_inner + gn].astype(BF16)
    cm = xbc[:, d_inner + gn:].astype(BF16)

    dt = _softplus(dt_ref[0, rows, :] + dtb_ref[...])
    a = dt * (-jnp.exp(alog_ref[...]))
    acum = _cumsum_time(tril_ref[...], a)
    acum_t = acum.T
    a_last = acum[L - 1:L, :]
    w_end = dt * jnp.exp(a_last - acum)
    stacked = jnp.concatenate(
        [dt, w_end, jnp.exp(acum), jnp.broadcast_to(jnp.exp(a_last), (8, LANES))], axis=0)
    hi = stacked.astype(BF16)
    lo = (stacked - hi.astype(F32)).astype(BF16)
    ex = _dot(jnp.concatenate([hi, lo], axis=1), expand_ref[...])
    dt_x, w_x, e_x, dec_x = ex[0:L], ex[L:2 * L], ex[2 * L:3 * L], ex[3 * L:3 * L + 1]

    xdt = (x * dt_x).astype(BF16)
    xw = (x * w_x).astype(BF16)

    row = lax.broadcasted_iota(jnp.int32, (L, L), 0)
    col = lax.broadcasted_iota(jnp.int32, (L, L), 1)
    causal = row >= col
    low_half = col < SSM_HEAD_DIM

    for g in range(SSM_GROUPS):
        bg = bm[:, g * SSM_STATE:(g + 1) * SSM_STATE]
        cg = cm[:, g * SSM_STATE:(g + 1) * SSM_STATE]
        gcols = slice(g * gw, (g + 1) * gw)
        cb = _dot_nt(cg, bg)
        state = state_ref[g]
        y_inter = _dot(cg, state.astype(BF16)) * e_x[:, gcols]
        tiles = []
        for hp in range(heads_per_group // 2):
            h0 = g * heads_per_group + 2 * hp
            cols = slice(h0 * SSM_HEAD_DIM, (h0 + 2) * SSM_HEAD_DIM)
            xd = xdt[:, cols]
            parts = []
            for h in (h0, h0 + 1):
                seg = acum[:, h:h + 1] - acum_t[h:h + 1, :]
                decay = jnp.exp(jnp.where(causal, seg, NEG_BIG))
                parts.append(_dot((cb * decay).astype(BF16), xd))
            y = jnp.where(low_half, parts[0], parts[1])
            y = y + y_inter[:, hp * LANES:(hp + 1) * LANES] + dskip_ref[:, cols] * x[:, cols]
            zc = z_ref[0, rows, cols].astype(F32)
            tiles.append(y * (zc * _sigmoid(zc)))
        state_ref[g] = state * dec_x[:, gcols] + _dot_tn(bg, xw[:, gcols])
        ss = tiles[0] * tiles[0]
        for tl in tiles[1:]:
            ss = ss + tl * tl
        scale = lax.rsqrt(jnp.sum(ss, axis=-1, keepdims=True) * (1.0 / gw) + NORM_EPS)
        for hp, tl in enumerate(tiles):
            c0 = g * gw + hp * LANES
            y_ref[0, rows, c0:c0 + LANES] = (tl * scale * yg_ref[:, c0:c0 + LANES]).astype(y_ref.dtype)


def _cumsum_time(tril, a):
    acc = None
    r = a
    for _ in range(3):
        hi = r.astype(BF16)
        d = _dot(tril, hi)
        acc = d if acc is None else acc + d
        r = r - hi.astype(F32)
    return acc


def _mamba_ssd(z, xbc, dt, conv_w, conv_b, dt_bias, a_log, d_skip, y_g, *, n_heads):
    b, s, d_inner = z.shape
    conv_dim = xbc.shape[-1]
    hpg = n_heads // SSM_GROUPS
    gw = hpg * SSM_HEAD_DIM
    L = SSD_CHUNK

    pad = LANES - n_heads
    dtb = jnp.pad(dt_bias.astype(F32), (0, pad)).reshape(1, LANES)
    alog = jnp.pad(a_log.astype(F32), (0, pad)).reshape(1, LANES)
    dskip = jnp.repeat(d_skip.astype(F32), SSM_HEAD_DIM).reshape(1, d_inner)
    expand = (jnp.arange(LANES)[:, None] == (jnp.arange(d_inner)[None, :] // SSM_HEAD_DIM)).astype(BF16)
    expand = jnp.concatenate([expand, expand], axis=0)
    tril = (jnp.arange(L)[:, None] >= jnp.arange(L)[None, :]).astype(BF16)
    t_idx = jnp.arange(L)[:, None]
    shift = jnp.stack([(jnp.arange(2 * L)[None, :] == L + t_idx - (CONV_K - 1 - j)).astype(BF16)
                       for j in range(CONV_K - 1)])

    step = L * SSD_CHUNKS_PER_STEP
    blk = lambda n: pl.BlockSpec((1, step, n), lambda i, c: (i, c, 0))
    kern = functools.partial(_mamba_ssd_kernel, d_inner=d_inner, heads_per_group=hpg)
    return pl.pallas_call(
        kern,
        grid=(b, s // step),
        in_specs=[blk(d_inner), blk(conv_dim), blk(LANES),
                  _full(conv_w.shape), _full((1, conv_dim)), _full((1, LANES)), _full((1, LANES)),
                  _full((1, d_inner)), _full((1, d_inner)), _full(expand.shape), _full(tril.shape),
                  _full(shift.shape)],
        out_specs=blk(d_inner),
        out_shape=jax.ShapeDtypeStruct((b, s, d_inner), BF16),
        scratch_shapes=[pltpu.VMEM((2 * L, conv_dim), BF16),
                        pltpu.VMEM((SSM_GROUPS, SSM_STATE, gw), F32)],
        compiler_params=_params(("parallel", "arbitrary")),
        name="mamba_ssd",
    )(z, xbc, dt, conv_w.astype(F32), conv_b.astype(F32).reshape(1, conv_dim), dtb, alog, dskip,
      y_g.astype(F32).reshape(1, d_inner), expand, tril, shift)


def _layer_tail_kernel(y_ref, h_ref, p_ref, wo_ref, g_ref, wg_ref, wp_ref, o_ref):
    h1 = h_ref[...] + _dot(y_ref[...], wo_ref[...])
    u = (h1 * _rms_scale(h1) * g_ref[...]).astype(BF16)
    gate = _sigmoid(_dot(u, wg_ref[...]))
    pe = _dot(p_ref[...].astype(BF16), wp_ref[...])
    o_ref[...] = h1 + pe * gate


def _layer_tail(y2, h2, p2, w_out, ple_g, w_gate, w_proj):
    t, d = h2.shape
    row = lambda n: pl.BlockSpec((ROW_TILE, n), lambda i: (i, 0))
    return pl.pallas_call(
        _layer_tail_kernel,
        grid=(t // ROW_TILE,),
        in_specs=[row(y2.shape[1]), row(d), row(p2.shape[1]), _full(w_out.shape), _full((1, d)),
                  _full(w_gate.shape), _full(w_proj.shape)],
        out_specs=row(d),
        out_shape=jax.ShapeDtypeStruct((t, d), F32),
        compiler_params=_params(("parallel",)),
        name="layer_tail",
    )(y2, h2, p2, w_out.astype(BF16), ple_g.astype(F32).reshape(1, d), w_gate.astype(BF16),
      w_proj.astype(BF16))


def _head_norm(x, gain):
    rows, width = x.shape
    low = lax.broadcasted_iota(jnp.int32, (rows, LANES), 1) < SB_HEAD_DIM
    out = []
    for j in range(width // LANES):
        blk = x[:, j * LANES:(j + 1) * LANES]
        sq = blk * blk
        s_lo = jnp.sum(jnp.where(low, sq, 0.0), axis=-1, keepdims=True)
        s_hi = jnp.sum(jnp.where(low, 0.0, sq), axis=-1, keepdims=True)
        ms = jnp.where(low, s_lo, s_hi) * (1.0 / SB_HEAD_DIM)
        out.append(blk * lax.rsqrt(ms + NORM_EPS) * gain[:, j * LANES:(j + 1) * LANES])
    return out


def _attn_in_proj_kernel(h_ref, gkv_ref, gq_ref, wk_ref, wv_ref, wq_ref, wgate_ref, kg_ref, qg_ref,
                         q_ref, k_ref, v_ref, gate_ref):
    h = h_ref[...]
    hn = h * _rms_scale(h)
    ukv = (hn * gkv_ref[...]).astype(BF16)
    uq = (hn * gq_ref[...]).astype(BF16)
    v_ref[...] = _dot(ukv, wv_ref[...]).astype(v_ref.dtype)
    gate_ref[...] = _dot(uq, wgate_ref[...]).astype(gate_ref.dtype)
    for j, blk in enumerate(_head_norm(_dot(ukv, wk_ref[...]), kg_ref[...])):
        k_ref[:, j * LANES:(j + 1) * LANES] = blk.astype(k_ref.dtype)
    for j, blk in enumerate(_head_norm(_dot(uq, wq_ref[...]), qg_ref[...])):
        q_ref[:, j * LANES:(j + 1) * LANES] = blk.astype(q_ref.dtype)


def _attn_in_proj(h2, kv_norm, s_norm, w_kv, s_in, k_norm, q_norm):
    t, d = h2.shape
    width = w_kv.shape[1] // 2
    n_heads = width // SB_HEAD_DIM
    vec = lambda v: v.astype(F32).reshape(1, -1)
    kg = vec(jnp.tile(k_norm, n_heads))
    qg = vec(jnp.tile(q_norm * (1.0 / math.sqrt(SB_HEAD_DIM)), n_heads))
    wk, wv = w_kv[:, :width].astype(BF16), w_kv[:, width:].astype(BF16)
    wq, wgate = s_in[:, :width].astype(BF16), s_in[:, width:].astype(BF16)
    row = lambda n: pl.BlockSpec((ROW_TILE, n), lambda i: (i, 0))
    out = jax.ShapeDtypeStruct((t, width), BF16)
    return pl.pallas_call(
        _attn_in_proj_kernel,
        grid=(t // ROW_TILE,),
        in_specs=[row(d), _full((1, d)), _full((1, d)), _full(wk.shape), _full(wv.shape),
                  _full(wq.shape), _full(wgate.shape), _full((1, width)), _full((1, width))],
        out_specs=[row(width)] * 4,
        out_shape=[out] * 4,
        compiler_params=_params(("parallel",)),
        name="attn_in_proj",
    )(h2, vec(kv_norm), vec(s_norm), wk, wv, wq, wgate, kg, qg)


def _sb_attention_kernel(q_ref, k_ref, v_ref, gate_ref, tri_ref, o_ref, acc_ref):
    tq = ATTN_TILE
    n_heads = 2 * ATTN_LANE_TILES
    qi = pl.program_id(2)
    low = lax.broadcasted_iota(jnp.int32, (tq, LANES), 1) < SB_HEAD_DIM
    q_heads = []
    for lt in range(ATTN_LANE_TILES):
        q2 = q_ref[0, :, lt * LANES:(lt + 1) * LANES]
        zero = jnp.zeros_like(q2)
        q_heads += [jnp.where(low, q2, zero), jnp.where(low, zero, q2)]
    tri = tri_ref[...]
    row = lax.broadcasted_iota(jnp.int32, (tq, tq), 0)
    col = lax.broadcasted_iota(jnp.int32, (tq, tq), 1)
    strict = col < row

    lanes = [slice((h // 2) * LANES, (h // 2 + 1) * LANES) for h in range(n_heads)]

    def tile_step(tiles, carry, first):
        rows = [pl.ds(pl.multiple_of(kt * tq, tq), tq) for kt, _, _ in tiles]
        units = [(j, h) for j in range(len(tiles)) for h in range(n_heads)]
        diag = {u: tiles[u[0]][1] for u in units}
        z = {u: _dot_nt(q_heads[u[1]], k_ref[0, rows[u[0]], lanes[u[1]]]) for u in units}
        sp = {u: jnp.maximum(z[u], 0.0) + jnp.log(1.0 + jnp.exp2(jnp.abs(z[u]) * -LOG2E)) for u in units}
        sp = {u: jnp.where(strict, sp[u], 0.0) if diag[u] else sp[u] for u in units}
        total = {u: jnp.sum(sp[u], axis=-1, keepdims=True) for u in units}
        later = {u: _dot(sp[u].astype(BF16), tri) for u in units}
        logw = {u: (z[u] - sp[u]) - later[u] for u in units}
        logw = {u: jnp.where(strict, logw[u], NEG_BIG) if diag[u] else logw[u] for u in units}
        pv = {u: _dot(jnp.exp(logw[u]).astype(BF16), v_ref[0, rows[u[0]], lanes[u[1]]]) for u in units}
        carry = list(carry)
        for h in range(n_heads):
            contrib = None
            for j, (_, _, valid) in enumerate(tiles):
                c = jnp.exp(carry[h]) * pv[(j, h)]
                t = total[(j, h)]
                if valid is not None:
                    c = jnp.where(valid, c, 0.0)
                    t = jnp.where(valid, t, 0.0)
                contrib = c if contrib is None else contrib + c
                carry[h] = carry[h] - t
            if first:
                acc_ref[h] = contrib
            else:
                acc_ref[h] += contrib
        return tuple(carry), jnp.max(functools.reduce(jnp.maximum, carry))

    def keep_going(state):
        i, live, _ = state
        return jnp.logical_and(i < qi, live > EXP_UNDERFLOW)

    def walk(state):
        i, _, carry = state
        carry, live = tile_step([(qi - 1 - i, False, None)], carry, False)
        return i + 1, live, carry

    zeros = jnp.zeros((tq, 1), F32)
    carry, live = tile_step([(qi, True, None), (jnp.maximum(qi - 1, 0), False, qi > 0)],
                            (zeros,) * n_heads, True)
    lax.while_loop(keep_going, walk, (jnp.int32(1), live, carry))

    for lt in range(ATTN_LANE_TILES):
        lanes = slice(lt * LANES, (lt + 1) * LANES)
        o = jnp.where(low, acc_ref[2 * lt], acc_ref[2 * lt + 1])
        gt = gate_ref[0, :, lanes].astype(F32)
        o_ref[0, :, lanes] = (o * (gt * _sigmoid(gt))).astype(o_ref.dtype)


def _sb_attention(q, k, v, gate):
    b, s, width = q.shape
    tq = ATTN_TILE
    gw = ATTN_LANE_TILES * LANES
    tri = (jnp.arange(tq)[:, None] > jnp.arange(tq)[None, :]).astype(BF16)
    qblk = pl.BlockSpec((1, tq, gw), lambda i, hg, t: (i, t, hg))
    kvblk = pl.BlockSpec((1, s, gw), lambda i, hg, t: (i, 0, hg))
    return pl.pallas_call(
        _sb_attention_kernel,
        grid=(b, width // gw, s // tq),
        in_specs=[qblk, kvblk, kvblk, qblk, _full(tri.shape)],
        out_specs=qblk,
        out_shape=jax.ShapeDtypeStruct((b, s, width), BF16),
        scratch_shapes=[pltpu.VMEM((2 * ATTN_LANE_TILES, tq, LANES), F32)],
        compiler_params=_params(("parallel", "parallel", "arbitrary")),
        name="sb_attention",
    )(q, k, v, gate, tri)


def kernel(x, p, m_norm, m_in, m_conv_w, m_conv_b, m_dt_bias, m_A_log, m_D, m_ynorm, m_out, kv_norm, w_kv,
           k_norm, s_norm, s_in, q_norm, s_out, ple_norm, ple_gate, ple_proj):
    b, s, d = x.shape
    t = b * s
    n_a = m_norm.shape[0]
    depth = p.shape[0]
    n_heads = m_dt_bias.shape[1]
    d_inner = n_heads * SSM_HEAD_DIM
    conv_dim = m_conv_w.shape[2]

    h = x.reshape(t, d)
    kv = None
    for i in range(depth):
        if i < n_a:
            w_in = m_in[i]
            wz = w_in[:, :d_inner].astype(BF16)
            wx = w_in[:, d_inner:d_inner + conv_dim].astype(BF16)
            wdt = jnp.pad(w_in[:, d_inner + conv_dim:], ((0, 0), (0, LANES - n_heads))).astype(BF16)
            z, xbc, dt = _mamba_in_proj(h, m_norm[i].astype(F32).reshape(1, d), wz, wx, wdt)
            y = _mamba_ssd(z.reshape(b, s, -1), xbc.reshape(b, s, -1), dt.reshape(b, s, -1),
                           m_conv_w[i], m_conv_b[i], m_dt_bias[i], m_A_log[i], m_D[i], m_ynorm[i],
                           n_heads=n_heads)
            y = y.reshape(t, d_inner)
            w_out = m_out[i]
        else:
            j = i - n_a
            q, k, v, gate = _attn_in_proj(h, kv_norm, s_norm[j], w_kv, s_in[j], k_norm, q_norm[j])
            if kv is None:
                kv = (k, v)
            k, v = kv
            width = q.shape[1]
            y = _sb_attention(q.reshape(b, s, width), k.reshape(b, s, width), v.reshape(b, s, width),
                              gate.reshape(b, s, width)).reshape(t, width)
            w_out = s_out[j]
        h = _layer_tail(y, h, p[i].reshape(t, -1), w_out, ple_norm[i], ple_gate[i], ple_proj[i])
    return h.reshape(b, s, d)
```

```python
import functools
import math

import jax
import jax.numpy as jnp
from jax import lax
from jax.experimental import pallas as pl
from jax.experimental.pallas import tpu as pltpu

F32 = jnp.float32
BF16 = jnp.bfloat16

NORM_EPS = 1e-6
LANES = 128
SSD_CHUNK = 128
SSD_CHUNKS_PER_STEP = 4
SSM_HEAD_DIM = 64
SSM_STATE = 128
SSM_GROUPS = 4
CONV_K = 4
SB_HEAD_DIM = 64
NEG_BIG = -1e30
VMEM_LIMIT = 56 * 1024 * 1024

ROW_TILE = 512
ATTN_TILE = 256
ATTN_LANE_TILES = 4
LOG2E = 1.4426950408889634
EXP_UNDERFLOW = -105.0


def _sigmoid(x):
    return 1.0 / (1.0 + jnp.exp2(x * -LOG2E))


def _softplus(x):
    return jnp.maximum(x, 0.0) + jnp.log(1.0 + jnp.exp(-jnp.abs(x)))


def _rms_scale(x):
    return lax.rsqrt(jnp.mean(x * x, axis=-1, keepdims=True) + NORM_EPS)


def _dot(a, b):
    return jnp.dot(a, b, preferred_element_type=F32)


def _dot_nt(a, b):
    return lax.dot_general(a, b, (((1,), (1,)), ((), ())), preferred_element_type=F32)


def _dot_tn(a, b):
    return lax.dot_general(a, b, (((0,), (0,)), ((), ())), preferred_element_type=F32)


def _full(shape):
    return pl.BlockSpec(shape, lambda *_: (0,) * len(shape))


def _params(semantics):
    return pltpu.CompilerParams(dimension_semantics=semantics, vmem_limit_bytes=VMEM_LIMIT)


def _mamba_in_proj_kernel(x_ref, g_ref, wz_ref, wx_ref, wdt_ref, z_ref, xbc_ref, dt_ref):
    x = x_ref[...]
    u = (x * _rms_scale(x) * g_ref[...]).astype(BF16)
    z_ref[...] = _dot(u, wz_ref[...]).astype(z_ref.dtype)
    xbc_ref[...] = _dot(u, wx_ref[...]).astype(xbc_ref.dtype)
    dt_ref[...] = _dot(u, wdt_ref[...])


def _mamba_in_proj(x2, g, wz, wx, wdt):
    t, d = x2.shape
    row = lambda n: pl.BlockSpec((ROW_TILE, n), lambda i: (i, 0))
    return pl.pallas_call(
        _mamba_in_proj_kernel,
        grid=(t // ROW_TILE,),
        in_specs=[row(d), _full(g.shape), _full(wz.shape), _full(wx.shape), _full(wdt.shape)],
        out_specs=[row(wz.shape[1]), row(wx.shape[1]), row(wdt.shape[1])],
        out_shape=[jax.ShapeDtypeStruct((t, wz.shape[1]), BF16),
                   jax.ShapeDtypeStruct((t, wx.shape[1]), BF16),
                   jax.ShapeDtypeStruct((t, wdt.shape[1]), F32)],
        compiler_params=_params(("parallel",)),
        name="mamba_in_proj",
    )(x2, g, wz, wx, wdt)


def _mamba_ssd_kernel(z_ref, xbc_ref, dt_ref, cw_ref, cb_ref, dtb_ref, alog_ref, dskip_ref, yg_ref,
                      expand_ref, tril_ref, shift_ref, y_ref, ext_ref, state_ref, *, d_inner,
                      heads_per_group):
    L = SSD_CHUNK
    gn = SSM_GROUPS * SSM_STATE
    gw = heads_per_group * SSM_HEAD_DIM

    @pl.when(pl.program_id(1) == 0)
    def _():
        ext_ref[0:L, :] = jnp.zeros((L, ext_ref.shape[1]), ext_ref.dtype)
        state_ref[...] = jnp.zeros(state_ref.shape, F32)

    lax.fori_loop(0, SSD_CHUNKS_PER_STEP, functools.partial(
        _ssd_chunk, z_ref, xbc_ref, dt_ref, cw_ref, cb_ref, dtb_ref, alog_ref, dskip_ref, yg_ref, expand_ref,
        tril_ref, shift_ref, y_ref, ext_ref, state_ref, d_inner, heads_per_group), None)


def _ssd_chunk(z_ref, xbc_ref, dt_ref, cw_ref, cb_ref, dtb_ref, alog_ref, dskip_ref, yg_ref, expand_ref,
               tril_ref, shift_ref, y_ref, ext_ref, state_ref, d_inner, heads_per_group, c, _):
    L = SSD_CHUNK
    gn = SSM_GROUPS * SSM_STATE
    gw = heads_per_group * SSM_HEAD_DIM
    rows = pl.ds(pl.multiple_of(c * L, L), L)

    cur = xbc_ref[0, rows, :]
    ext_ref[L:2 * L, :] = cur
    conv = cb_ref[...] + cw_ref[CONV_K - 1:CONV_K, :] * cur.astype(F32)
    delayed = _dot(shift_ref[...], ext_ref[...])
    for j in range(CONV_K - 1):
        conv = conv + cw_ref[j:j + 1, :] * delayed[j * L:(j + 1) * L]
    ext_ref[0:L, :] = cur
    xbc = conv * _sigmoid(conv)
    x = xbc[:, :d_inner]
    bm = xbc[:, d_inner:d_inner + gn].astype(BF16)
    cm = xbc[:, d_inner + gn:].astype(BF16)

    dt = _softplus(dt_ref[0, rows, :] + dtb_ref[...])
    a = dt * (-jnp.exp(alog_ref[...]))
    acum = _cumsum_time(tril_ref[...], a)
    acum_t = acum.T
    a_last = acum[L - 1:L, :]
    w_end = dt * jnp.exp(a_last - acum)
    stacked = jnp.concatenate(
        [dt, w_end, jnp.exp(acum), jnp.broadcast_to(jnp.exp(a_last), (8, LANES))], axis=0)
    hi = stacked.astype(BF16)
    lo = (stacked - hi.astype(F32)).astype(BF16)
    ex = _dot(jnp.concatenate([hi, lo], axis=1), expand_ref[...])
    dt_x, w_x, e_x, dec_x = ex[0:L], ex[L:2 * L], ex[2 * L:3 * L], ex[3 * L:3 * L + 1]

    xdt = (x * dt_x).astype(BF16)
    xw = (x * w_x).astype(BF16)

    row = lax.broadcasted_iota(jnp.int32, (L, L), 0)
    col = lax.broadcasted_iota(jnp.int32, (L, L), 1)
    causal = row >= col
    low_half = col < SSM_HEAD_DIM

    for g in range(SSM_GROUPS):
        bg = bm[:, g * SSM_STATE:(g + 1) * SSM_STATE]
        cg = cm[:, g * SSM_STATE:(g + 1) * SSM_STATE]
        gcols = slice(g * gw, (g + 1) * gw)
        cb = _dot_nt(cg, bg)
        state = state_ref[g]
        y_inter = _dot(cg, state.astype(BF16)) * e_x[:, gcols]
        tiles = []
        for hp in range(heads_per_group // 2):
            h0 = g * heads_per_group + 2 * hp
            cols = slice(h0 * SSM_HEAD_DIM, (h0 + 2) * SSM_HEAD_DIM)
            xd = xdt[:, cols]
            parts = []
            for h in (h0, h0 + 1):
                seg = acum[:, h:h + 1] - acum_t[h:h + 1, :]
                decay = jnp.exp(jnp.where(causal, seg, NEG_BIG))
                parts.append(_dot((cb * decay).astype(BF16), xd))
            y = jnp.where(low_half, parts[0], parts[1])
            y = y + y_inter[:, hp * LANES:(hp + 1) * LANES] + dskip_ref[:, cols] * x[:, cols]
            zc = z_ref[0, rows, cols].astype(F32)
            tiles.append(y * (zc * _sigmoid(zc)))
        state_ref[g] = state * dec_x[:, gcols] + _dot_tn(bg, xw[:, gcols])
        ss = tiles[0] * tiles[0]
        for tl in tiles[1:]:
            ss = ss + tl * tl
        scale = lax.rsqrt(jnp.sum(ss, axis=-1, keepdims=True) * (1.0 / gw) + NORM_EPS)
        for hp, tl in enumerate(tiles):
            c0 = g * gw + hp * LANES
            y_ref[0, rows, c0:c0 + LANES] = (tl * scale * yg_ref[:, c0:c0 + LANES]).astype(y_ref.dtype)


def _cumsum_time(tril, a):
    acc = None
    r = a
    for _ in range(3):
        hi = r.astype(BF16)
        d = _dot(tril, hi)
        acc = d if acc is None else acc + d
        r = r - hi.astype(F32)
    return acc


def _mamba_ssd(z, xbc, dt, conv_w, conv_b, dt_bias, a_log, d_skip, y_g, *, n_heads):
    b, s, d_inner = z.shape
    conv_dim = xbc.shape[-1]
    hpg = n_heads // SSM_GROUPS
    gw = hpg * SSM_HEAD_DIM
    L = SSD_CHUNK

    pad = LANES - n_heads
    dtb = jnp.pad(dt_bias.astype(F32), (0, pad)).reshape(1, LANES)
    alog = jnp.pad(a_log.astype(F32), (0, pad)).reshape(1, LANES)
    dskip = jnp.repeat(d_skip.astype(F32), SSM_HEAD_DIM).reshape(1, d_inner)
    expand = (jnp.arange(LANES)[:, None] == (jnp.arange(d_inner)[None, :] // SSM_HEAD_DIM)).astype(BF16)
    expand = jnp.concatenate([expand, expand], axis=0)
    tril = (jnp.arange(L)[:, None] >= jnp.arange(L)[None, :]).astype(BF16)
    t_idx = jnp.arange(L)[:, None]
    shift = jnp.concatenate([(jnp.arange(2 * L)[None, :] == L + t_idx - (CONV_K - 1 - j)).astype(BF16)
                             for j in range(CONV_K - 1)], axis=0)

    step = L * SSD_CHUNKS_PER_STEP
    blk = lambda n: pl.BlockSpec((1, step, n), lambda i, c: (i, c, 0))
    kern = functools.partial(_mamba_ssd_kernel, d_inner=d_inner, heads_per_group=hpg)
    return pl.pallas_call(
        kern,
        grid=(b, s // step),
        in_specs=[blk(d_inner), blk(conv_dim), blk(LANES),
                  _full(conv_w.shape), _full((1, conv_dim)), _full((1, LANES)), _full((1, LANES)),
                  _full((1, d_inner)), _full((1, d_inner)), _full(expand.shape), _full(tril.shape),
                  _full(shift.shape)],
        out_specs=blk(d_inner),
        out_shape=jax.ShapeDtypeStruct((b, s, d_inner), BF16),
        scratch_shapes=[pltpu.VMEM((2 * L, conv_dim), BF16),
                        pltpu.VMEM((SSM_GROUPS, SSM_STATE, gw), F32)],
        compiler_params=_params(("parallel", "arbitrary")),
        name="mamba_ssd",
    )(z, xbc, dt, conv_w.astype(F32), conv_b.astype(F32).reshape(1, conv_dim), dtb, alog, dskip,
      y_g.astype(F32).reshape(1, d_inner), expand, tril, shift)


def _layer_tail_kernel(y_ref, h_ref, p_ref, wo_ref, g_ref, wg_ref, wp_ref, o_ref):
    h1 = h_ref[...] + _dot(y_ref[...], wo_ref[...])
    u = (h1 * _rms_scale(h1) * g_ref[...]).astype(BF16)
    gate = _sigmoid(_dot(u, wg_ref[...]))
    pe = _dot(p_ref[...].astype(BF16), wp_ref[...])
    o_ref[...] = h1 + pe * gate


def _layer_tail(y2, h2, p2, w_out, ple_g, w_gate, w_proj):
    t, d = h2.shape
    row = lambda n: pl.BlockSpec((ROW_TILE, n), lambda i: (i, 0))
    return pl.pallas_call(
        _layer_tail_kernel,
        grid=(t // ROW_TILE,),
        in_specs=[row(y2.shape[1]), row(d), row(p2.shape[1]), _full(w_out.shape), _full((1, d)),
                  _full(w_gate.shape), _full(w_proj.shape)],
        out_specs=row(d),
        out_shape=jax.ShapeDtypeStruct((t, d), F32),
        compiler_params=_params(("parallel",)),
        name="layer_tail",
    )(y2, h2, p2, w_out.astype(BF16), ple_g.astype(F32).reshape(1, d), w_gate.astype(BF16),
      w_proj.astype(BF16))


def _head_norm(x, gain):
    rows, width = x.shape
    low = lax.broadcasted_iota(jnp.int32, (rows, LANES), 1) < SB_HEAD_DIM
    out = []
    for j in range(width // LANES):
        blk = x[:, j * LANES:(j + 1) * LANES]
        sq = blk * blk
        s_lo = jnp.sum(jnp.where(low, sq, 0.0), axis=-1, keepdims=True)
        s_hi = jnp.sum(jnp.where(low, 0.0, sq), axis=-1, keepdims=True)
        ms = jnp.where(low, s_lo, s_hi) * (1.0 / SB_HEAD_DIM)
        out.append(blk * lax.rsqrt(ms + NORM_EPS) * gain[:, j * LANES:(j + 1) * LANES])
    return out


def _attn_in_proj_kernel(h_ref, gkv_ref, gq_ref, wk_ref, wv_ref, wq_ref, wgate_ref, kg_ref, qg_ref,
                         q_ref, k_ref, v_ref, gate_ref):
    h = h_ref[...]
    hn = h * _rms_scale(h)
    ukv = (hn * gkv_ref[...]).astype(BF16)
    uq = (hn * gq_ref[...]).astype(BF16)
    v_ref[...] = _dot(ukv, wv_ref[...]).astype(v_ref.dtype)
    gate_ref[...] = _dot(uq, wgate_ref[...]).astype(gate_ref.dtype)
    for j, blk in enumerate(_head_norm(_dot(ukv, wk_ref[...]), kg_ref[...])):
        k_ref[:, j * LANES:(j + 1) * LANES] = blk.astype(k_ref.dtype)
    for j, blk in enumerate(_head_norm(_dot(uq, wq_ref[...]), qg_ref[...])):
        q_ref[:, j * LANES:(j + 1) * LANES] = blk.astype(q_ref.dtype)


def _attn_in_proj(h2, kv_norm, s_norm, w_kv, s_in, k_norm, q_norm):
    t, d = h2.shape
    width = w_kv.shape[1] // 2
    n_heads = width // SB_HEAD_DIM
    vec = lambda v: v.astype(F32).reshape(1, -1)
    kg = vec(jnp.tile(k_norm, n_heads))
    qg = vec(jnp.tile(q_norm * (1.0 / math.sqrt(SB_HEAD_DIM)), n_heads))
    wk, wv = w_kv[:, :width].astype(BF16), w_kv[:, width:].astype(BF16)
    wq, wgate = s_in[:, :width].astype(BF16), s_in[:, width:].astype(BF16)
    row = lambda n: pl.BlockSpec((ROW_TILE, n), lambda i: (i, 0))
    out = jax.ShapeDtypeStruct((t, width), BF16)
    return pl.pallas_call(
        _attn_in_proj_kernel,
        grid=(t // ROW_TILE,),
        in_specs=[row(d), _full((1, d)), _full((1, d)), _full(wk.shape), _full(wv.shape),
                  _full(wq.shape), _full(wgate.shape), _full((1, width)), _full((1, width))],
        out_specs=[row(width)] * 4,
        out_shape=[out] * 4,
        compiler_params=_params(("parallel",)),
        name="attn_in_proj",
    )(h2, vec(kv_norm), vec(s_norm), wk, wv, wq, wgate, kg, qg)


def _sb_attention_kernel(q_ref, k_ref, v_ref, gate_ref, tri_ref, o_ref, acc_ref):
    tq = ATTN_TILE
    n_heads = 2 * ATTN_LANE_TILES
    qi = pl.program_id(2)
    low = lax.broadcasted_iota(jnp.int32, (tq, LANES), 1) < SB_HEAD_DIM
    q_heads = []
    for lt in range(ATTN_LANE_TILES):
        q2 = q_ref[0, :, lt * LANES:(lt + 1) * LANES]
        zero = jnp.zeros_like(q2)
        q_heads += [jnp.where(low, q2, zero), jnp.where(low, zero, q2)]
    tri = tri_ref[...]
    row = lax.broadcasted_iota(jnp.int32, (tq, tq), 0)
    col = lax.broadcasted_iota(jnp.int32, (tq, tq), 1)
    strict = col < row

    lanes = [slice((h // 2) * LANES, (h // 2 + 1) * LANES) for h in range(n_heads)]

    def tile_step(tiles, carry, first):
        rows = [pl.ds(pl.multiple_of(kt * tq, tq), tq) for kt, _, _ in tiles]
        units = [(j, h) for j in range(len(tiles)) for h in range(n_heads)]
        diag = {u: tiles[u[0]][1] for u in units}
        z = {u: _dot_nt(q_heads[u[1]], k_ref[0, rows[u[0]], lanes[u[1]]]) for u in units}
        sp = {u: jnp.maximum(z[u], 0.0) + jnp.log(1.0 + jnp.exp2(jnp.abs(z[u]) * -LOG2E)) for u in units}
        sp = {u: jnp.where(strict, sp[u], 0.0) if diag[u] else sp[u] for u in units}
        total = {u: jnp.sum(sp[u], axis=-1, keepdims=True) for u in units}
        later = {u: _dot(sp[u].astype(BF16), tri) for u in units}
        logw = {u: (z[u] - sp[u]) - later[u] for u in units}
        logw = {u: jnp.where(strict, logw[u], NEG_BIG) if diag[u] else logw[u] for u in units}
        pv = {u: _dot(jnp.exp(logw[u]).astype(BF16), v_ref[0, rows[u[0]], lanes[u[1]]]) for u in units}
        carry = list(carry)
        for h in range(n_heads):
            contrib = None
            for j, (_, _, valid) in enumerate(tiles):
                c = jnp.exp(carry[h]) * pv[(j, h)]
                t = total[(j, h)]
                if valid is not None:
                    c = jnp.where(valid, c, 0.0)
                    t = jnp.where(valid, t, 0.0)
                contrib = c if contrib is None else contrib + c
                carry[h] = carry[h] - t
            if first:
                acc_ref[h] = contrib
            else:
                acc_ref[h] += contrib
        return tuple(carry), jnp.max(functools.reduce(jnp.maximum, carry))

    def keep_going(state):
        i, live, _ = state
        return jnp.logical_and(i < qi, live > EXP_UNDERFLOW)

    def walk(state):
        i, _, carry = state
        carry, live = tile_step([(qi - 1 - i, False, None)], carry, False)
        return i + 1, live, carry

    zeros = jnp.zeros((tq, 1), F32)
    carry, live = tile_step([(qi, True, None), (jnp.maximum(qi - 1, 0), False, qi > 0)],
                            (zeros,) * n_heads, True)
    lax.while_loop(keep_going, walk, (jnp.int32(1), live, carry))

    for lt in range(ATTN_LANE_TILES):
        lanes = slice(lt * LANES, (lt + 1) * LANES)
        o = jnp.where(low, acc_ref[2 * lt], acc_ref[2 * lt + 1])
        gt = gate_ref[0, :, lanes].astype(F32)
        o_ref[0, :, lanes] = (o * (gt * _sigmoid(gt))).astype(o_ref.dtype)


def _sb_attention(q, k, v, gate):
    b, s, width = q.shape
    tq = ATTN_TILE
    gw = ATTN_LANE_TILES * LANES
    tri = (jnp.arange(tq)[:, None] > jnp.arange(tq)[None, :]).astype(BF16)
    qblk = pl.BlockSpec((1, tq, gw), lambda i, hg, t: (i, t, hg))
    kvblk = pl.BlockSpec((1, s, gw), lambda i, hg, t: (i, 0, hg))
    return pl.pallas_call(
        _sb_attention_kernel,
        grid=(b, width // gw, s // tq),
        in_specs=[qblk, kvblk, kvblk, qblk, _full(tri.shape)],
        out_specs=qblk,
        out_shape=jax.ShapeDtypeStruct((b, s, width), BF16),
        scratch_shapes=[pltpu.VMEM((2 * ATTN_LANE_TILES, tq, LANES), F32)],
        compiler_params=_params(("parallel", "parallel", "arbitrary")),
        name="sb_attention",
    )(q, k, v, gate, tri)


def kernel(x, p, m_norm, m_in, m_conv_w, m_conv_b, m_dt_bias, m_A_log, m_D, m_ynorm, m_out, kv_norm, w_kv,
           k_norm, s_norm, s_in, q_norm, s_out, ple_norm, ple_gate, ple_proj):
    b, s, d = x.shape
    t = b * s
    n_a = m_norm.shape[0]
    depth = p.shape[0]
    n_heads = m_dt_bias.shape[1]
    d_inner = n_heads * SSM_HEAD_DIM
    conv_dim = m_conv_w.shape[2]

    h = x.reshape(t, d)
    kv = None
    for i in range(depth):
        if i < n_a:
            w_in = m_in[i]
            wz = w_in[:, :d_inner].astype(BF16)
            wx = w_in[:, d_inner:d_inner + conv_dim].astype(BF16)
            wdt = jnp.pad(w_in[:, d_inner + conv_dim:], ((0, 0), (0, LANES - n_heads))).astype(BF16)
            z, xbc, dt = _mamba_in_proj(h, m_norm[i].astype(F32).reshape(1, d), wz, wx, wdt)
            y = _mamba_ssd(z.reshape(b, s, -1), xbc.reshape(b, s, -1), dt.reshape(b, s, -1),
                           m_conv_w[i], m_conv_b[i], m_dt_bias[i], m_A_log[i], m_D[i], m_ynorm[i],
                           n_heads=n_heads)
            y = y.reshape(t, d_inner)
            w_out = m_out[i]
        else:
            j = i - n_a
            q, k, v, gate = _attn_in_proj(h, kv_norm, s_norm[j], w_kv, s_in[j], k_norm, q_norm[j])
            if kv is None:
                kv = (k, v)
            k, v = kv
            width = q.shape[1]
            y = _sb_attention(q.reshape(b, s, width), k.reshape(b, s, width), v.reshape(b, s, width),
                              gate.reshape(b, s, width)).reshape(t, width)
            w_out = s_out[j]
        h = _layer_tail(y, h, p[i].reshape(t, -1), w_out, ple_norm[i], ple_gate[i], ple_proj[i])
    return h.reshape(b, s, d)
```
